```python
import math
import jax
import jax.numpy as jnp
from jax import lax
import numpy as np

D_MODEL = 2048
BATCH = 8
SEQ = 2048
DEPTH = 1
DEC_BATCH = 32
DEC_SEQ = 1
PAST_LEN = 16384
PAGE_SIZE = 128

HG_HEADS = 8
HG_KDIM = 128
HG_VDIM = 128
HG_WIDTH = HG_HEADS * HG_VDIM
HG_CHUNK = 64
AT_HEADS = 8
AT_QK_DIM = 64
AT_V_DIM = 2 * AT_QK_DIM
AT_WIDTH = AT_HEADS * AT_V_DIM
MIX_WIDTH = HG_WIDTH + AT_WIDTH
ROT_DIM = AT_QK_DIM // 4
ROPE_THETA = 500000.0
Q_BLOCK = 128
D_FF = 5632
CONV_W = 3
N_MOD = 6
EPS = 1e-6
IN_SIZES = (HG_HEADS * HG_KDIM, HG_HEADS * HG_KDIM, HG_WIDTH, HG_WIDTH,
            AT_HEADS * 2 * AT_QK_DIM, AT_HEADS * 2 * AT_QK_DIM, AT_WIDTH)
IN_WIDTH = sum(IN_SIZES)

kernel_name = 'hymba_hgrn2_diffattn_convffn_step'


def rms_norm(x, g):
    xf = x.astype(jnp.float32)
    y = xf * lax.rsqrt(jnp.mean(xf * xf, axis=-1, keepdims=True) + EPS)
    return (y * g.astype(jnp.float32)).astype(x.dtype)


def adaln(c, w_ada_l, b_ada_l):
    return jnp.split(jax.nn.silu(c) @ w_ada_l + b_ada_l, N_MOD, axis=-1)


def modulate(x, g, shift, scale):
    return rms_norm(x, g) * (1 + scale[:, None, :]) + shift[:, None, :]


def split_cols(z):
    out, start = [], 0
    for n in IN_SIZES:
        out.append(z[..., start:start + n])
        start += n
    return out


def rope_partial(x, pos):
    half = ROT_DIM // 2
    inv_freq = ROPE_THETA ** (-jnp.arange(half, dtype=jnp.float32) / half)
    ang = pos.astype(jnp.float32)[:, None] * inv_freq[None, :]
    cos = jnp.cos(ang)[:, None, None, :]
    sin = jnp.sin(ang)[:, None, None, :]
    xr = x[..., :ROT_DIM].astype(jnp.float32)
    x1, x2 = xr[..., :half], xr[..., half:]
    rot = jnp.concatenate([x1 * cos - x2 * sin, x2 * cos + x1 * sin], axis=-1)
    return jnp.concatenate([rot.astype(x.dtype), x[..., ROT_DIM:]], axis=-1)


def project_mixers(h, pos, lb, w_in_l):
    B, S, _ = h.shape
    hq, hf, hi, hg, aq, ak, av = split_cols(h @ w_in_l)
    f = lb + (1.0 - lb) * jax.nn.sigmoid(hf.astype(jnp.float32))

    def heads(t, d):
        return t.astype(jnp.float32).reshape(B, S, HG_HEADS, d).transpose(0, 2, 1, 3)

    q_hg = heads(hq, HG_KDIM)
    k_hg = heads(1.0 - f, HG_KDIM)
    v_hg = heads(hi, HG_VDIM)
    lf_hg = heads(jnp.log(f), HG_KDIM)
    q = rope_partial(aq.reshape(B, S, AT_HEADS, 2, AT_QK_DIM), pos)
    k = rope_partial(ak.reshape(B, S, AT_HEADS, 2, AT_QK_DIM), pos)
    v = av.reshape(B, S, AT_HEADS, AT_V_DIM)
    return q_hg, k_hg, v_hg, lf_hg, hg, q, k, v


def hgrn_chunked(q, k, v, log_f):
    B, H, S, dk = q.shape
    dv = v.shape[-1]
    nc = S // HG_CHUNK

    def chunks(t):
        return t.reshape(B, H, nc, HG_CHUNK, t.shape[-1]).transpose(2, 0, 1, 3, 4)

    mask = jnp.tril(jnp.ones((HG_CHUNK, HG_CHUNK), dtype=bool))[:, :, None]

    def step(state, xs):
        qc, kc, vc, lc = xs
        b = jnp.cumsum(lc, axis=-2)
        o = jnp.einsum('bhtk,bhkv->bhtv', qc * jnp.exp(b), state)
        diff = b[:, :, :, None, :] - b[:, :, None, :, :]
        decay = jnp.where(mask, jnp.exp(jnp.where(mask, diff, 0.0)), 0.0)
        att = jnp.einsum('bhtk,bhsk,bhtsk->bhts', qc, kc, decay)
        o = o + jnp.einsum('bhts,bhsv->bhtv', att, vc)
        b_last = b[:, :, -1:, :]
        state = (jnp.exp(b_last[:, :, 0, :])[..., None] * state
                 + jnp.einsum('bhsk,bhsv->bhkv', kc * jnp.exp(b_last - b), vc))
        return state, o

    s0 = jnp.zeros((B, H, dk, dv), jnp.float32)
    s_fin, o = lax.scan(step, s0, (chunks(q), chunks(k), chunks(v), chunks(log_f)))
    return o.transpose(1, 2, 0, 3, 4).reshape(B, H, S, dv), s_fin


def hgrn_recurrent(q, k, v, log_f, state0):
    def step(state, xs):
        qt, kt, vt, lt = xs
        state = jnp.exp(lt)[..., None] * state + kt[..., None] * vt[..., None, :]
        return state, jnp.einsum('bhk,bhkv->bhv', qt, state)

    xs = (q.transpose(2, 0, 1, 3), k.transpose(2, 0, 1, 3),
          v.transpose(2, 0, 1, 3), log_f.transpose(2, 0, 1, 3))
    s_fin, o = lax.scan(step, state0.astype(jnp.float32), xs)
    return o.transpose(1, 2, 0, 3), s_fin


def diff_attn_prompt(q, k, v, lam):
    B, S, H, _, d = q.shape
    nb = S // Q_BLOCK
    scale = AT_QK_DIM ** -0.5
    kf = k.astype(jnp.float32)
    vf = v.astype(jnp.float32)
    kpos = jnp.arange(S)
    qb = q.astype(jnp.float32).reshape(B, nb, Q_BLOCK, H, 2, d).transpose(1, 0, 2, 3, 4, 5)

    def block(args):
        qblk, start = args
        s = jnp.einsum('bqhcd,bkhcd->bhcqk', qblk, kf) * scale
        qpos = start + jnp.arange(Q_BLOCK)
        s = jnp.where(qpos[:, None] >= kpos[None, :], s, -jnp.inf)
        p = jax.nn.softmax(s, axis=-1)
        w = p[:, :, 0] - lam * p[:, :, 1]
        return jnp.einsum('bhqk,bkhv->bqhv', w, vf)

    o = lax.map(block, (qb, jnp.arange(nb) * Q_BLOCK))
    return o.transpose(1, 0, 2, 3, 4).reshape(B, S, H, AT_V_DIM)


def diff_attn_sample(q, k_new, v_new, lam, cache_k, cache_v, page_table, layer):
    DB, T, H, _, d = q.shape
    qf = q.astype(jnp.float32) * (AT_QK_DIM ** -0.5)
    s = jnp.einsum('bqhcd,bkhcd->bhcqk', qf, k_new.astype(jnp.float32))
    s = jnp.where(jnp.tril(jnp.ones((T, T), dtype=bool)), s, -jnp.inf)
    m = jnp.max(s, axis=-1)
    p = jnp.exp(s - m[..., None])
    l = jnp.sum(p, axis=-1)
    acc = jnp.einsum('bhcqk,bkhv->bhcqv', p, v_new.astype(jnp.float32))

    def step(carry, phys):
        m, l, acc = carry
        kp = cache_k[layer, phys].astype(jnp.float32).reshape(DB, PAGE_SIZE, H, 2, d)
        vp = cache_v[layer, phys].astype(jnp.float32)
        s = jnp.einsum('bqhcd,bkhcd->bhcqk', qf, kp)
        m_new = jnp.maximum(m, jnp.max(s, axis=-1))
        corr = jnp.exp(m - m_new)
        p = jnp.exp(s - m_new[..., None])
        l = l * corr + jnp.sum(p, axis=-1)
        acc = acc * corr[..., None] + jnp.einsum('bhcqk,bkhv->bhcqv', p, vp)
        return (m_new, l, acc), None

    (m, l, acc), _ = lax.scan(step, (m, l, acc), page_table.T)
    o = acc / l[..., None]
    w = o[:, :, 0] - lam * o[:, :, 1]
    return w.transpose(0, 2, 1, 3)


def merge_mixers(o_hg, g, o_at, lam_init, hg_norm_l, at_norm_l, w_out_l, dtype):
    B, _, S, _ = o_hg.shape
    y_hg = rms_norm(o_hg.transpose(0, 2, 1, 3), hg_norm_l).reshape(B, S, HG_WIDTH) * jax.nn.silu(g.astype(jnp.float32))
    y_at = rms_norm(o_at, at_norm_l).reshape(B, S, AT_WIDTH) * (1.0 - lam_init)
    return jnp.concatenate([y_hg, y_at], axis=-1).astype(dtype) @ w_out_l


def conv_ffn(h, conv_prev, w_up_l, conv_w_l, conv_b_l, w_down_l):
    S = h.shape[1]
    a, u = jnp.split(h @ w_up_l, 2, axis=-1)
    ext = jnp.concatenate([conv_prev.astype(a.dtype), a], axis=1)
    conv = conv_b_l + ext[:, :S] * conv_w_l[0]
    for j in range(1, CONV_W):
        conv = conv + ext[:, j:j + S] * conv_w_l[j]
    return (jax.nn.silu(conv) * u) @ w_down_l, ext[:, S:]


def group_layer(x, c, pos, lb, lam, lam_init, hgrn_fn, attn_fn, conv_prev,
                w_ada_l, b_ada_l, norm1_l, norm2_l, w_in_l, hg_norm_l, at_norm_l,
                w_out_l, w_up_l, conv_w_l, conv_b_l, w_down_l):
    B, S, _ = x.shape
    sh1, sc1, g1, sh2, sc2, g2 = adaln(c, w_ada_l, b_ada_l)
    h = modulate(x, norm1_l, sh1, sc1)
    q_hg, k_hg, v_hg, lf_hg, gate_hg, q, k, v = project_mixers(h, pos, lb, w_in_l)
    o_hg, s_hg = hgrn_fn(q_hg, k_hg, v_hg, lf_hg)
    o_at = attn_fn(q, k, v, lam)
    x = x + g1[:, None, :] * merge_mixers(o_hg, gate_hg, o_at, lam_init, hg_norm_l, at_norm_l, w_out_l, x.dtype)
    h2 = modulate(x, norm2_l, sh2, sc2)
    ff, conv_tail = conv_ffn(h2, conv_prev, w_up_l, conv_w_l, conv_b_l, w_down_l)
    x = x + g2[:, None, :] * ff
    return x, k.reshape(B, S, AT_HEADS, 2 * AT_QK_DIM), v, s_hg, conv_tail


def setup_inputs(seed: int = 0) -> dict:
    key = jax.random.key(seed)
    ks = jax.random.split(key, 32)
    n_pages = PAST_LEN // PAGE_SIZE
    n_pool = (5 * DEC_BATCH * n_pages) // 4
    f32 = jnp.float32

    def nrm(k, shape, s):
        return jax.random.normal(k, shape, f32) * s

    page_table = jax.random.permutation(ks[8], n_pool)[:DEC_BATCH * n_pages].reshape(DEC_BATCH, n_pages).astype(jnp.int32)
    return {
        'x_prompt': nrm(ks[0], (BATCH, SEQ, D_MODEL), 1.0),
        'x_sample': nrm(ks[1], (DEC_BATCH, DEC_SEQ, D_MODEL), 1.0),
        'c_prompt': nrm(ks[2], (BATCH, D_MODEL), 1.0),
        'c_sample': nrm(ks[3], (DEC_BATCH, D_MODEL), 1.0),
        'cache_k': nrm(ks[4], (DEPTH, n_pool, PAGE_SIZE, AT_HEADS, 2 * AT_QK_DIM), 1.0),
        'cache_v': nrm(ks[5], (DEPTH, n_pool, PAGE_SIZE, AT_HEADS, AT_V_DIM), 1.0),
        'state_hgrn': nrm(ks[6], (DEPTH, DEC_BATCH, HG_HEADS, HG_KDIM, HG_VDIM), 0.5),
        'state_conv': nrm(ks[7], (DEPTH, DEC_BATCH, CONV_W - 1, D_FF), 1.0),
        'page_table': page_table,
        'w_ada': nrm(ks[9], (DEPTH, D_MODEL, N_MOD * D_MODEL), 0.5 * D_MODEL ** -0.5),
        'b_ada': nrm(ks[10], (DEPTH, N_MOD * D_MODEL), 0.02),
        'norm1': 1.0 + nrm(ks[11], (DEPTH, D_MODEL), 0.02),
        'norm2': 1.0 + nrm(ks[12], (DEPTH, D_MODEL), 0.02),
        'w_in': nrm(ks[13], (DEPTH, D_MODEL, IN_WIDTH), D_MODEL ** -0.5),
        'hg_lb_logits': nrm(ks[14], (DEPTH + 1, HG_HEADS * HG_KDIM), 0.1),
        'hg_norm': 1.0 + nrm(ks[15], (DEPTH, HG_VDIM), 0.02),
        'lam_q1': nrm(ks[16], (DEPTH, AT_QK_DIM), 0.1),
        'lam_k1': nrm(ks[17], (DEPTH, AT_QK_DIM), 0.1),
        'lam_q2': nrm(ks[18], (DEPTH, AT_QK_DIM), 0.1),
        'lam_k2': nrm(ks[19], (DEPTH, AT_QK_DIM), 0.1),
        'at_norm': 1.0 + nrm(ks[20], (DEPTH, AT_V_DIM), 0.02),
        'w_out': nrm(ks[21], (DEPTH, MIX_WIDTH, D_MODEL), MIX_WIDTH ** -0.5),
        'w_up': nrm(ks[22], (DEPTH, D_MODEL, 2 * D_FF), D_MODEL ** -0.5),
        'conv_w': nrm(ks[23], (DEPTH, CONV_W, D_FF), 0.5),
        'conv_b': nrm(ks[24], (DEPTH, D_FF), 0.02),
        'w_down': nrm(ks[25], (DEPTH, D_FF, D_MODEL), D_FF ** -0.5),
        'final_norm': 1.0 + nrm(ks[26], (D_MODEL,), 0.02),
    }


def reference(x_prompt, x_sample, c_prompt, c_sample, cache_k, cache_v, state_hgrn, state_conv, page_table,
              w_ada, b_ada, norm1, norm2, w_in, hg_lb_logits, hg_norm, lam_q1, lam_k1, lam_q2, lam_k2,
              at_norm, w_out, w_up, conv_w, conv_b, w_down, final_norm):
    past_len = page_table.shape[1] * cache_k.shape[2]
    pos_p = jnp.arange(x_prompt.shape[1], dtype=jnp.int32)
    pos_s = past_len + jnp.arange(x_sample.shape[1], dtype=jnp.int32)
    lb_all = jnp.cumsum(jax.nn.softmax(hg_lb_logits.astype(jnp.float32), axis=0), axis=0)
    yp, ys = x_prompt, x_sample
    kp_l, vp_l, ks_l, vs_l, hp_l, hs_l, cp_l, cs_l = [], [], [], [], [], [], [], []
    for l in range(DEPTH):
        lam_init = 0.8 - 0.6 * math.exp(-0.3 * l)
        lam = (jnp.exp(jnp.sum(lam_q1[l].astype(jnp.float32) * lam_k1[l].astype(jnp.float32)))
               - jnp.exp(jnp.sum(lam_q2[l].astype(jnp.float32) * lam_k2[l].astype(jnp.float32))) + lam_init)
        w_l = (w_ada[l], b_ada[l], norm1[l], norm2[l], w_in[l], hg_norm[l], at_norm[l],
               w_out[l], w_up[l], conv_w[l], conv_b[l], w_down[l])
        conv0 = jnp.zeros((yp.shape[0], CONV_W - 1, D_FF), yp.dtype)
        yp, kp, vp, hp, cp = group_layer(yp, c_prompt, pos_p, lb_all[l], lam, lam_init,
                                         hgrn_chunked, diff_attn_prompt, conv0, *w_l)
        hgrn_fn = lambda q, k, v, lf, s0=state_hgrn[l]: hgrn_recurrent(q, k, v, lf, s0)
        attn_fn = lambda q, k, v, lm, li=l: diff_attn_sample(q, k, v, lm, cache_k, cache_v, page_table, li)
        ys, ksm, vsm, hs, cs = group_layer(ys, c_sample, pos_s, lb_all[l], lam, lam_init,
                                           hgrn_fn, attn_fn, state_conv[l], *w_l)
        kp_l.append(kp); vp_l.append(vp); ks_l.append(ksm); vs_l.append(vsm)
        hp_l.append(hp); hs_l.append(hs); cp_l.append(cp); cs_l.append(cs)
    y_prompt = rms_norm(yp, final_norm)
    y_sample = rms_norm(ys, final_norm)
    return (y_prompt, y_sample, jnp.stack(kp_l), jnp.stack(vp_l), jnp.stack(ks_l), jnp.stack(vs_l),
            jnp.stack(hp_l), jnp.stack(hs_l), jnp.stack(cp_l), jnp.stack(cs_l))
```

```python
import functools
import math

import jax
import jax.numpy as jnp
from jax import lax
from jax.experimental import pallas as pl
from jax.experimental.pallas import tpu as pltpu

EPS = 1e-6
ROPE_THETA = 500000.0
N_MOD = 6
LANES = 128
HGRN_CHUNK = 64
HGRN_SUB = 16
HALO = 16
VMEM_LIMIT_BYTES = 56 * 1024 * 1024

F32 = jnp.float32
BF16 = jnp.bfloat16


def _params(*sem):
    return pltpu.CompilerParams(dimension_semantics=sem, vmem_limit_bytes=VMEM_LIMIT_BYTES)


def _silu(x):
    return x / (1.0 + jnp.exp(-x))


def _sigmoid(x):
    return 1.0 / (1.0 + jnp.exp(-x))


def _rms(x):
    return x * lax.rsqrt(jnp.mean(x * x, axis=-1, keepdims=True) + EPS)


def _dot(a, b):
    return jnp.dot(a, b, preferred_element_type=F32)


def _dot_nt(a, b):
    return lax.dot_general(a, b, (((1,), (1,)), ((), ())), preferred_element_type=F32)


def _lam_value(lq1, lk1, lq2, lk2, lam_init):
    return (jnp.exp(jnp.sum(lq1 * lk1, axis=-1, keepdims=True))
            - jnp.exp(jnp.sum(lq2 * lk2, axis=-1, keepdims=True)) + lam_init)


def _lower_bound(layer_logits):
    mx = functools.reduce(jnp.maximum, layer_logits)
    es = [jnp.exp(r - mx) for r in layer_logits]
    return es[0] / functools.reduce(jnp.add, es)


def _adaln_kernel(c_ref, w_ref, b_ref, o_ref):
    s = _silu(c_ref[...]).astype(BF16)
    o_ref[...] = _dot(s, w_ref[...].astype(BF16)) + b_ref[...]


def _adaln(c_all, w_ada, b_ada, tn=512):
    rows, d = c_all.shape
    n = w_ada.shape[1]
    assert n % tn == 0
    return pl.pallas_call(
        _adaln_kernel,
        grid=(n // tn,),
        in_specs=[pl.BlockSpec((rows, d), lambda j: (0, 0)),
                  pl.BlockSpec((d, tn), lambda j: (0, j)),
                  pl.BlockSpec((1, tn), lambda j: (0, j))],
        out_specs=pl.BlockSpec((rows, tn), lambda j: (0, j)),
        out_shape=jax.ShapeDtypeStruct((rows, n), F32),
        compiler_params=_params("arbitrary"),
        name="adaln",
    )(c_all, w_ada, b_ada.reshape(1, n))


def _rope(blk, cos_t, sin_a, sin_b):
    return (blk * cos_t + pltpu.roll(blk, 8, axis=1) * sin_a
            + pltpu.roll(blk, LANES - 8, axis=1) * sin_b)


def _inproj_kernel(x_ref, sh_ref, sc_ref, g_ref, w_ref, cos_ref, sa_ref, sb_ref,
                   zq_ref, k_ref, v_ref, h_scr, *, n_heads):
    j = pl.program_id(1)

    @pl.when(j == 0)
    def _():
        h = _rms(x_ref[...]) * g_ref[...] * (1.0 + sc_ref[...]) + sh_ref[...]
        h_scr[...] = h.astype(BF16)

    acc = _dot(h_scr[...], w_ref[...])

    def roped(dst):
        for hh in range(n_heads):
            sl = slice(hh * LANES, (hh + 1) * LANES)
            dst[:, sl] = _rope(acc[:, sl], cos_ref[...], sa_ref[...], sb_ref[...])

    @pl.when(j < 4)
    def _():
        zq_ref[...] = acc

    @pl.when(j == 4)
    def _():
        roped(zq_ref)

    @pl.when(j == 5)
    def _():
        roped(k_ref)

    @pl.when(j == 6)
    def _():
        v_ref[...] = acc


def _inproj(x2, sh, sc, mod_spec, norm_g, w_in, tables, tab_spec, tm, n_heads):
    m, d = x2.shape
    tn = n_heads * LANES
    n_tiles = w_in.shape[1] // tn
    assert n_tiles == 7
    sh_spec, sc_spec = mod_spec
    return pl.pallas_call(
        functools.partial(_inproj_kernel, n_heads=n_heads),
        grid=(m // tm, n_tiles),
        in_specs=[pl.BlockSpec((tm, d), lambda i, j: (i, 0)),
                  sh_spec, sc_spec,
                  pl.BlockSpec((1, d), lambda i, j: (0, 0)),
                  pl.BlockSpec((d, tn), lambda i, j: (0, j)),
                  tab_spec, tab_spec, tab_spec],
        out_specs=[pl.BlockSpec((None, tm, tn), lambda i, j: (jnp.minimum(j, 4), i, 0)),
                   pl.BlockSpec((tm, tn), lambda i, j: (i, 0)),
                   pl.BlockSpec((tm, tn), lambda i, j: (i, 0))],
        out_shape=[jax.ShapeDtypeStruct((5, m, tn), F32),
                   jax.ShapeDtypeStruct((m, tn), F32),
                   jax.ShapeDtypeStruct((m, tn), F32)],
        scratch_shapes=[pltpu.VMEM((tm, d), BF16)],
        compiler_params=_params("arbitrary", "arbitrary"),
        name="inproj",
    )(x2, sh, sc, norm_g, w_in, *tables)


def _hgrn_prompt_kernel(q_ref, f_ref, v_ref, g_ref, lbl_ref, gn_ref, y_ref, sfin_ref, st_scr,
                        *, rows):
    t = pl.program_id(2)
    c_rows, sub = HGRN_CHUNK, HGRN_SUB

    @pl.when(t == 0)
    def _():
        st_scr[...] = jnp.zeros_like(st_scr)

    lb = _lower_bound([lbl_ref[l:l + 1, :] for l in range(lbl_ref.shape[0])])
    ri = lax.broadcasted_iota(jnp.int32, (c_rows, c_rows), 0)
    ci = lax.broadcasted_iota(jnp.int32, (c_rows, c_rows), 1)
    tri = jnp.where(ri >= ci, 1.0, 0.0).astype(BF16)
    row = lax.broadcasted_iota(jnp.int32, (c_rows, 1), 0)
    row_in_sub = row % sub

    def chunk(c, carry):
        r0 = pl.multiple_of(c * c_rows, c_rows)
        q = q_ref[pl.ds(r0, c_rows), :]
        v = v_ref[pl.ds(r0, c_rows), :]
        f = lb + (1.0 - lb) * _sigmoid(f_ref[pl.ds(r0, c_rows), :])
        kk = 1.0 - f
        lf = jnp.log(f)
        hi = lf.astype(BF16)
        r1 = lf - hi.astype(F32)
        mid = r1.astype(BF16)
        lo = (r1 - mid.astype(F32)).astype(BF16)
        b = _dot(tri, hi) + _dot(tri, mid) + _dot(tri, lo)
        b_last = b[c_rows - 1:c_rows, :]

        st = st_scr[...]
        o = _dot_nt((q * jnp.exp(b)).astype(BF16), st.astype(BF16))
        kd = kk * jnp.exp(b_last - b)
        st_scr[...] = st * jnp.exp(b_last) + _dot(v.T.astype(BF16), kd.astype(BF16))

        v16 = v.astype(BF16)
        for jb in range(c_rows // sub - 1):
            e = (jb + 1) * sub
            r_ref = b[e - 1:e, :]
            qj = q * jnp.exp(jnp.minimum(b - r_ref, 0.0))
            kj = kk[jb * sub:e, :] * jnp.exp(r_ref - b[jb * sub:e, :])
            att = _dot_nt(qj.astype(BF16), kj.astype(BF16))
            att = jnp.where(row >= e, att, 0.0)
            o = o + _dot(att.astype(BF16), v16[jb * sub:e, :])
        for dlt in range(sub):
            if dlt == 0:
                w = q * kk
                vs = v
            else:
                ks = pltpu.roll(kk, dlt, axis=0)
                bs = pltpu.roll(b, dlt, axis=0)
                vs = pltpu.roll(v, dlt, axis=0)
                w = q * ks * jnp.exp(b - bs)
            a = jnp.sum(w, axis=-1, keepdims=True)
            a = jnp.where(row_in_sub >= dlt, a, 0.0)
            o = o + a * vs

        y = _rms(o) * gn_ref[...] * _silu(g_ref[pl.ds(r0, c_rows), :])
        y_ref[pl.ds(r0, c_rows), :] = y
        return carry

    lax.fori_loop(0, rows // c_rows, chunk, 0)

    @pl.when(t == pl.num_programs(2) - 1)
    def _():
        sfin_ref[...] = st_scr[...].T


def _hgrn_prompt(zq, lb_logits, hg_norm, n_batch, seq, n_heads, rows=512):
    m = zq.shape[1]
    nt = seq // rows

    def zspec(part):
        return pl.BlockSpec((None, rows, LANES), lambda b, h, t: (part, b * nt + t, h))

    return pl.pallas_call(
        functools.partial(_hgrn_prompt_kernel, rows=rows),
        grid=(n_batch, n_heads, nt),
        in_specs=[zspec(0), zspec(1), zspec(2), zspec(3),
                  pl.BlockSpec((lb_logits.shape[0], LANES), lambda b, h, t: (0, h)),
                  pl.BlockSpec((1, LANES), lambda b, h, t: (0, 0))],
        out_specs=[pl.BlockSpec((rows, LANES), lambda b, h, t: (b * nt + t, h)),
                   pl.BlockSpec((None, None, LANES, LANES), lambda b, h, t: (b, h, 0, 0))],
        out_shape=[jax.ShapeDtypeStruct((m, n_heads * LANES), F32),
                   jax.ShapeDtypeStruct((n_batch, n_heads, LANES, LANES), F32)],
        scratch_shapes=[pltpu.VMEM((LANES, LANES), F32)],
        compiler_params=_params("arbitrary", "arbitrary", "arbitrary"),
        name="hgrn_prompt",
    )(zq, zq, zq, zq, lb_logits, hg_norm)


def _hgrn_sample_kernel(q_ref, f_ref, v_ref, g_ref, lbl_ref, gn_ref, s_ref, y_ref, so_ref,
                        *, n_heads):
    lb = _lower_bound([lbl_ref[l] for l in range(lbl_ref.shape[0])])
    q = q_ref[...]
    v = v_ref[...]
    f = lb + (1.0 - lb) * _sigmoid(f_ref[...])
    kk = 1.0 - f
    pad = jnp.zeros((LANES - 3 * n_heads, LANES), F32)
    cols = jnp.concatenate([f, kk, q, pad], axis=0).T
    o_rows = []
    for h in range(n_heads):
        f_col = cols[:, h:h + 1]
        k_col = cols[:, n_heads + h:n_heads + h + 1]
        q_col = cols[:, 2 * n_heads + h:2 * n_heads + h + 1]
        s_new = f_col * s_ref[h] + k_col * v[h:h + 1, :]
        so_ref[h] = s_new
        o_rows.append(jnp.sum(q_col * s_new, axis=0, keepdims=True))
    o = jnp.concatenate(o_rows, axis=0)
    y_ref[...] = _rms(o) * gn_ref[...] * _silu(g_ref[...])


def _hgrn_sample(zq4, lb_logits3, hg_norm, state):
    n_seq, n_heads = state.shape[0], state.shape[1]

    def zspec(part):
        return pl.BlockSpec((None, None, n_heads, LANES), lambda b: (part, b, 0, 0))

    return pl.pallas_call(
        functools.partial(_hgrn_sample_kernel, n_heads=n_heads),
        grid=(n_seq,),
        in_specs=[zspec(0), zspec(1), zspec(2), zspec(3),
                  pl.BlockSpec(lb_logits3.shape, lambda b: (0, 0, 0)),
                  pl.BlockSpec((1, LANES), lambda b: (0, 0)),
                  pl.BlockSpec((None, n_heads, LANES, LANES), lambda b: (b, 0, 0, 0))],
        out_specs=[pl.BlockSpec((None, n_heads, LANES), lambda b: (b, 0, 0)),
                   pl.BlockSpec((None, n_heads, LANES, LANES), lambda b: (b, 0, 0, 0))],
        out_shape=[jax.ShapeDtypeStruct((n_seq, n_heads, LANES), F32),
                   jax.ShapeDtypeStruct(state.shape, F32)],
        compiler_params=_params("arbitrary"),
        name="hgrn_sample",
    )(zq4, zq4, zq4, zq4, lb_logits3, hg_norm, state)


def _attn_prompt_kernel(lq1_ref, lk1_ref, lq2_ref, lk2_ref, an_ref, q_ref, k_ref, v_ref, y_ref,
                        m_scr, l_scr, acc_scr, *, lam_init, tile, qk_dim):
    qi = pl.program_id(2)
    kj = pl.program_id(3)

    @pl.when(kj == 0)
    def _():
        m_scr[...] = jnp.full_like(m_scr, -jnp.inf)
        l_scr[...] = jnp.zeros_like(l_scr)
        acc_scr[...] = jnp.zeros_like(acc_scr)

    def step(on_diagonal):
        q = q_ref[...] * (qk_dim ** -0.5)
        lane = lax.broadcasted_iota(jnp.int32, (1, LANES), 1)
        q_maps = (jnp.where(lane < qk_dim, q, 0.0).astype(BF16),
                  jnp.where(lane >= qk_dim, q, 0.0).astype(BF16))
        k = k_ref[...].astype(BF16)
        v = v_ref[...].astype(BF16)
        if on_diagonal:
            causal = (lax.broadcasted_iota(jnp.int32, (tile, tile), 0)
                      >= lax.broadcasted_iota(jnp.int32, (tile, tile), 1))
        for c in range(2):
            s = _dot_nt(q_maps[c], k)
            if on_diagonal:
                s = jnp.where(causal, s, -jnp.inf)
            m_old = m_scr[c]
            m_new = jnp.maximum(m_old, jnp.max(s, axis=-1, keepdims=True))
            p = jnp.exp(s - m_new)
            corr = jnp.exp(m_old - m_new)
            l_scr[c] = l_scr[c] * corr + jnp.sum(p, axis=-1, keepdims=True)
            acc_scr[c] = acc_scr[c] * corr + _dot(p.astype(BF16), v)
            m_scr[c] = m_new

    @pl.when(kj < qi)
    def _():
        step(False)

    @pl.when(kj == qi)
    def _():
        step(True)
        lam = _lam_value(lq1_ref[...], lk1_ref[...], lq2_ref[...], lk2_ref[...], lam_init)
        o = acc_scr[0] / l_scr[0] - lam * (acc_scr[1] / l_scr[1])
        y_ref[...] = _rms(o) * an_ref[...] * (1.0 - lam_init)


def _attn_prompt(lam_params, at_norm, zq, k, v, n_batch, seq, n_heads, lam_init, tile=512):
    m = k.shape[0]
    nt = seq // tile
    qk_dim = lam_params[0].shape[1]
    small = [pl.BlockSpec(p.shape, lambda b, h, i, j: (0, 0)) for p in lam_params]

    def kv_spec():
        return pl.BlockSpec((tile, LANES), lambda b, h, i, j: (b * nt + jnp.minimum(j, i), h))

    return pl.pallas_call(
        functools.partial(_attn_prompt_kernel, lam_init=lam_init, tile=tile, qk_dim=qk_dim),
        grid=(n_batch, n_heads, nt, nt),
        in_specs=small + [pl.BlockSpec((1, LANES), lambda b, h, i, j: (0, 0)),
                          pl.BlockSpec((None, tile, LANES), lambda b, h, i, j: (4, b * nt + i, h)),
                          kv_spec(), kv_spec()],
        out_specs=pl.BlockSpec((tile, LANES), lambda b, h, i, j: (b * nt + i, h)),
        out_shape=jax.ShapeDtypeStruct((m, n_heads * LANES), F32),
        scratch_shapes=[pltpu.VMEM((2, tile, 1), F32), pltpu.VMEM((2, tile, 1), F32),
                        pltpu.VMEM((2, tile, LANES), F32)],
        compiler_params=_params("arbitrary", "arbitrary", "arbitrary", "arbitrary"),
        name="attn_prompt",
    )(*lam_params, at_norm, zq, k, v)


def _attn_sample_kernel(pt_ref, lq1_ref, lk1_ref, lq2_ref, lk2_ref, an_ref, q_ref, kn_ref, vn_ref,
                        *rest, lam_init, n_heads, qk_dim, pages):
    k_refs = rest[:pages]
    v_refs = rest[pages:2 * pages]
    y_ref, m_scr, l_scr, acc_scr = rest[2 * pages:]
    j = pl.program_id(1)
    page_rows = k_refs[0].shape[0] * n_heads

    q = q_ref[...] * (qk_dim ** -0.5)
    lane = lax.broadcasted_iota(jnp.int32, (1, LANES), 1)
    qm = jnp.concatenate([jnp.where(lane < qk_dim, q, 0.0), jnp.where(lane >= qk_dim, q, 0.0)], axis=0)

    @pl.when(j == 0)
    def _():
        kn = kn_ref[...]
        vn = vn_ref[...]
        m_scr[...] = jnp.sum(qm * jnp.concatenate([kn, kn], axis=0), axis=-1, keepdims=True)
        l_scr[...] = jnp.ones_like(l_scr)
        acc_scr[...] = jnp.concatenate([vn, vn], axis=0)

    qm16 = qm.astype(BF16)
    own_head = (lax.broadcasted_iota(jnp.int32, (2 * n_heads, page_rows), 1) % n_heads
                == lax.broadcasted_iota(jnp.int32, (2 * n_heads, page_rows), 0) % n_heads)
    for p in range(pages):
        k2 = k_refs[p][...].reshape(page_rows, LANES).astype(BF16)
        v2 = v_refs[p][...].reshape(page_rows, LANES).astype(BF16)
        s = jnp.where(own_head, _dot_nt(qm16, k2), -jnp.inf)
        m_old = m_scr[...]
        m_new = jnp.maximum(m_old, jnp.max(s, axis=-1, keepdims=True))
        pr = jnp.exp(s - m_new)
        corr = jnp.exp(m_old - m_new)
        l_scr[...] = l_scr[...] * corr + jnp.sum(pr, axis=-1, keepdims=True)
        acc_scr[...] = acc_scr[...] * corr + _dot(pr.astype(BF16), v2)
        m_scr[...] = m_new

    @pl.when(j == pl.num_programs(1) - 1)
    def _():
        lam = _lam_value(lq1_ref[...], lk1_ref[...], lq2_ref[...], lk2_ref[...], lam_init)
        o = acc_scr[...] / l_scr[...]
        w = o[:n_heads] - lam * o[n_heads:]
        y_ref[...] = _rms(w) * an_ref[...] * (1.0 - lam_init)


def _attn_sample(page_table, lam_params, at_norm, q3, kn3, vn3, cache_k, cache_v, lam_init, pages=8):
    n_seq, n_pages = page_table.shape
    _, _, page_size, n_heads, _ = cache_k.shape
    qk_dim = lam_params[0].shape[1]
    assert n_pages % pages == 0
    small = [pl.BlockSpec(p.shape, lambda b, j, pt: (0, 0)) for p in lam_params]
    row_spec = pl.BlockSpec((None, n_heads, LANES), lambda b, j, pt: (b, 0, 0))

    def page_spec(p):
        return pl.BlockSpec(
            (None, None, page_size, n_heads, LANES),
            lambda b, j, pt: (0, pt[b * n_pages + j * pages + p], 0, 0, 0))

    grid_spec = pltpu.PrefetchScalarGridSpec(
        num_scalar_prefetch=1,
        grid=(n_seq, n_pages // pages),
        in_specs=(small + [pl.BlockSpec((1, LANES), lambda b, j, pt: (0, 0)), row_spec, row_spec, row_spec]
                  + [page_spec(p) for p in range(pages)] + [page_spec(p) for p in range(pages)]),
        out_specs=row_spec,
        scratch_shapes=[pltpu.VMEM((2 * n_heads, 1), F32), pltpu.VMEM((2 * n_heads, 1), F32),
                        pltpu.VMEM((2 * n_heads, LANES), F32)],
    )
    return pl.pallas_call(
        functools.partial(_attn_sample_kernel, lam_init=lam_init, n_heads=n_heads, qk_dim=qk_dim,
                          pages=pages),
        grid_spec=grid_spec,
        out_shape=jax.ShapeDtypeStruct((n_seq, n_heads, LANES), F32),
        compiler_params=_params("arbitrary", "arbitrary"),
        name="attn_sample",
    )(page_table.reshape(-1), *lam_params, at_norm, q3, kn3, vn3,
      *([cache_k] * pages), *([cache_v] * pages))


def _outproj_kernel(x_ref, yh_ref, ya_ref, g1_ref, w_ref, o_ref):
    hw = yh_ref.shape[1]
    acc = _dot(yh_ref[...].astype(BF16), w_ref[:hw, :]) + _dot(ya_ref[...].astype(BF16), w_ref[hw:, :])
    o_ref[...] = x_ref[...] + g1_ref[...] * acc


def _outproj(x2, y_hg, y_at, g1, g1_spec, w_out, tm):
    m, d = x2.shape
    return pl.pallas_call(
        _outproj_kernel,
        grid=(m // tm,),
        in_specs=[pl.BlockSpec((tm, d), lambda i: (i, 0)),
                  pl.BlockSpec((tm, y_hg.shape[1]), lambda i: (i, 0)),
                  pl.BlockSpec((tm, y_at.shape[1]), lambda i: (i, 0)),
                  g1_spec,
                  pl.BlockSpec(w_out.shape, lambda i: (0, 0))],
        out_specs=pl.BlockSpec((tm, d), lambda i: (i, 0)),
        out_shape=jax.ShapeDtypeStruct((m, d), F32),
        compiler_params=_params("arbitrary"),
        name="outproj",
    )(x2, y_hg, y_at, g1, w_out)


def _ffn_prompt_kernel(x_ref, halo_ref, sh_ref, sc_ref, g2_ref, n2_ref, wa_ref, wu_ref, cw_ref, cb_ref,
                       wd_ref, fn_ref, y_ref, tail_ref, h_scr, a_scr, acc_scr, *, tm, tiles_per_seq):
    i = pl.program_id(0)
    f = pl.program_id(1)

    def modulated(x):
        return (_rms(x) * n2_ref[...] * (1.0 + sc_ref[...]) + sh_ref[...]).astype(BF16)

    @pl.when(f == 0)
    def _():
        h_scr[:HALO, :] = modulated(halo_ref[...])
        h_scr[HALO:, :] = modulated(x_ref[...])
        acc_scr[...] = jnp.zeros_like(acc_scr)

    a_scr[...] = _dot(h_scr[...], wa_ref[...])

    @pl.when(i % tiles_per_seq == 0)
    def _():
        a_scr[:HALO, :] = jnp.zeros((HALO, a_scr.shape[1]), F32)

    u = _dot(h_scr[HALO:, :], wu_ref[...])
    conv = (cb_ref[...] + a_scr[HALO - 2:HALO - 2 + tm, :] * cw_ref[0:1, :]
            + a_scr[HALO - 1:HALO - 1 + tm, :] * cw_ref[1:2, :]
            + a_scr[HALO:, :] * cw_ref[2:3, :])
    acc_scr[...] += _dot((_silu(conv) * u).astype(BF16), wd_ref[...])
    tail_ref[...] = a_scr[HALO + tm - 2:, :]

    @pl.when(f == pl.num_programs(1) - 1)
    def _():
        y_ref[...] = _rms(x_ref[...] + g2_ref[...] * acc_scr[...]) * fn_ref[...]


def _ffn_prompt(x1, mod_p, norm2, w_up, conv_w, conv_b, w_down, final_norm, n_batch, seq, tm=512, tf=512):
    m, d = x1.shape
    ff = w_down.shape[0]
    nf = ff // tf
    tps = seq // tm
    assert conv_w.shape[0] == 3

    def mspec(part):
        return pl.BlockSpec((None, 1, d), lambda i, f: (i // tps, 0, part))

    y, tails = pl.pallas_call(
        functools.partial(_ffn_prompt_kernel, tm=tm, tiles_per_seq=tps),
        grid=(m // tm, nf),
        in_specs=[pl.BlockSpec((tm, d), lambda i, f: (i, 0)),
                  pl.BlockSpec((HALO, d), lambda i, f: (jnp.maximum(i * (tm // HALO) - 1, 0), 0)),
                  mspec(3), mspec(4), mspec(5),
                  pl.BlockSpec((1, d), lambda i, f: (0, 0)),
                  pl.BlockSpec((d, tf), lambda i, f: (0, f)),
                  pl.BlockSpec((d, tf), lambda i, f: (0, nf + f)),
                  pl.BlockSpec((3, tf), lambda i, f: (0, f)),
                  pl.BlockSpec((1, tf), lambda i, f: (0, f)),
                  pl.BlockSpec((tf, d), lambda i, f: (f, 0)),
                  pl.BlockSpec((1, d), lambda i, f: (0, 0))],
        out_specs=[pl.BlockSpec((tm, d), lambda i, f: (i, 0)),
                   pl.BlockSpec((None, 2, tf), lambda i, f: (i, 0, f))],
        out_shape=[jax.ShapeDtypeStruct((m, d), F32),
                   jax.ShapeDtypeStruct((m // tm, 2, ff), F32)],
        scratch_shapes=[pltpu.VMEM((HALO + tm, d), BF16), pltpu.VMEM((HALO + tm, tf), F32),
                        pltpu.VMEM((tm, d), F32)],
        compiler_params=_params("arbitrary", "arbitrary"),
        name="ffn_prompt",
    )(x1, x1, mod_p, mod_p, mod_p, norm2, w_up, w_up, conv_w, conv_b, w_down, final_norm)
    return y, tails[tps - 1::tps]


def _ffn_sample_kernel(x_ref, sh_ref, sc_ref, g2_ref, n2_ref, p0_ref, p1_ref, wa_ref, wu_ref, cw_ref,
                       cb_ref, wd_ref, fn_ref, y_ref, a_ref, h_scr, acc_scr):
    f = pl.program_id(0)

    @pl.when(f == 0)
    def _():
        h = _rms(x_ref[...]) * n2_ref[...] * (1.0 + sc_ref[...]) + sh_ref[...]
        h_scr[...] = h.astype(BF16)
        acc_scr[...] = jnp.zeros_like(acc_scr)

    a = _dot(h_scr[...], wa_ref[...])
    u = _dot(h_scr[...], wu_ref[...])
    conv = (cb_ref[...] + p0_ref[...] * cw_ref[0:1, :] + p1_ref[...] * cw_ref[1:2, :]
            + a * cw_ref[2:3, :])
    acc_scr[...] += _dot((_silu(conv) * u).astype(BF16), wd_ref[...])
    a_ref[...] = a

    @pl.when(f == pl.num_programs(0) - 1)
    def _():
        y_ref[...] = _rms(x_ref[...] + g2_ref[...] * acc_scr[...]) * fn_ref[...]


def _ffn_sample(x1, mod_s, norm2, prev2, w_up, conv_w, conv_b, w_down, final_norm, tf=512):
    m, d = x1.shape
    ff = w_down.shape[0]
    nf = ff // tf

    def mspec(part):
        return pl.BlockSpec((m, d), lambda f: (0, part))

    return pl.pallas_call(
        _ffn_sample_kernel,
        grid=(nf,),
        in_specs=[pl.BlockSpec((m, d), lambda f: (0, 0)),
                  mspec(3), mspec(4), mspec(5),
                  pl.BlockSpec((1, d), lambda f: (0, 0)),
                  pl.BlockSpec((m, tf), lambda f: (0, f)),
                  pl.BlockSpec((m, tf), lambda f: (0, nf + f)),
                  pl.BlockSpec((d, tf), lambda f: (0, f)),
                  pl.BlockSpec((d, tf), lambda f: (0, nf + f)),
                  pl.BlockSpec((3, tf), lambda f: (0, f)),
                  pl.BlockSpec((1, tf), lambda f: (0, f)),
                  pl.BlockSpec((tf, d), lambda f: (f, 0)),
                  pl.BlockSpec((1, d), lambda f: (0, 0))],
        out_specs=[pl.BlockSpec((m, d), lambda f: (0, 0)),
                   pl.BlockSpec((m, tf), lambda f: (0, f))],
        out_shape=[jax.ShapeDtypeStruct((m, d), F32),
                   jax.ShapeDtypeStruct((m, ff), F32)],
        scratch_shapes=[pltpu.VMEM((m, d), BF16), pltpu.VMEM((m, d), F32)],
        compiler_params=_params("arbitrary"),
        name="ffn_sample",
    )(x1, mod_s, mod_s, mod_s, norm2, prev2, prev2, w_up, w_up, conv_w, conv_b, w_down, final_norm)


def _rope_tables(pos, qk_dim):
    rot = qk_dim // 4
    half = rot // 2
    inv_freq = ROPE_THETA ** (-jnp.arange(half, dtype=F32) / half)
    ang = pos.astype(F32)[:, None] * inv_freq[None, :]
    cos, sin = jnp.cos(ang), jnp.sin(ang)
    ones = jnp.ones((pos.shape[0], qk_dim - rot), F32)
    zeros = jnp.zeros((pos.shape[0], qk_dim - rot), F32)
    zh = jnp.zeros_like(sin)
    cos_t = jnp.concatenate([cos, cos, ones] * 2, axis=1)
    sin_a = jnp.concatenate([zh, sin, zeros] * 2, axis=1)
    sin_b = jnp.concatenate([-sin, zh, zeros] * 2, axis=1)
    return cos_t, sin_a, sin_b


def kernel(x_prompt, x_sample, c_prompt, c_sample, cache_k, cache_v, state_hgrn, state_conv, page_table,
           w_ada, b_ada, norm1, norm2, w_in, hg_lb_logits, hg_norm, lam_q1, lam_k1, lam_q2, lam_k2,
           at_norm, w_out, w_up, conv_w, conv_b, w_down, final_norm):
    n_batch, seq, d = x_prompt.shape
    n_seq = x_sample.shape[0]
    depth = w_in.shape[0]
    assert depth == 1 and x_sample.shape[1] == 1
    n_heads = state_hgrn.shape[2]
    assert cache_k.shape[3] == n_heads and cache_k.shape[4] == LANES and state_hgrn.shape[3] == LANES
    qk_dim = lam_q1.shape[1]
    past_len = page_table.shape[1] * cache_k.shape[2]
    lam_init = 0.8 - 0.6 * math.exp(-0.3 * 0)
    ff = w_down.shape[1]
    width = n_heads * LANES
    lam_params = [lam_q1, lam_k1, lam_q2, lam_k2]

    w_in16 = w_in[0].astype(BF16)
    w_out16 = w_out[0].astype(BF16)
    w_up16 = w_up[0].astype(BF16)
    w_down16 = w_down[0].astype(BF16)

    mod = _adaln(jnp.concatenate([c_prompt, c_sample], axis=0), w_ada[0], b_ada[0])
    mod_p = mod[:n_batch].reshape(n_batch, 1, N_MOD * d)
    mod_s = mod[n_batch:]

    tm = 512
    tps = seq // tm
    xp = x_prompt.reshape(n_batch * seq, d)
    tables_p = _rope_tables(jnp.arange(seq, dtype=jnp.int32), qk_dim)

    def pspec(part):
        return pl.BlockSpec((None, 1, d), lambda i, j: (i // tps, 0, part))

    zq, k_p, v_p = _inproj(xp, mod_p, mod_p, (pspec(0), pspec(1)), norm1, w_in16, tables_p,
                           pl.BlockSpec((tm, LANES), lambda i, j: (i % tps, 0)), tm, n_heads)
    y_hg, hgrn_p = _hgrn_prompt(zq, hg_lb_logits, hg_norm, n_batch, seq, n_heads)
    y_at = _attn_prompt(lam_params, at_norm, zq, k_p, v_p, n_batch, seq, n_heads, lam_init)
    x1 = _outproj(xp, y_hg, y_at, mod_p, pl.BlockSpec((None, 1, d), lambda i: (i // tps, 0, 2)),
                  w_out16, tm)
    y_p, conv_p = _ffn_prompt(x1, mod_p, norm2, w_up16, conv_w[0], conv_b, w_down16,
                              final_norm.reshape(1, d), n_batch, seq)

    xs = x_sample.reshape(n_seq, d)
    tables_s = _rope_tables(jnp.full((n_seq,), past_len, jnp.int32), qk_dim)

    def sspec(part):
        return pl.BlockSpec((n_seq, d), lambda i, j: (0, part))

    zq_s, k_s, v_s = _inproj(xs, mod_s, mod_s, (sspec(0), sspec(1)), norm1, w_in16, tables_s,
                             pl.BlockSpec((n_seq, LANES), lambda i, j: (0, 0)), n_seq, n_heads)
    zq_s4 = zq_s.reshape(5, n_seq, n_heads, LANES)
    y_hg_s, hgrn_s = _hgrn_sample(zq_s4, hg_lb_logits.reshape(-1, n_heads, LANES), hg_norm, state_hgrn[0])
    y_at_s = _attn_sample(page_table, lam_params, at_norm, zq_s4[4], k_s.reshape(n_seq, n_heads, LANES),
                          v_s.reshape(n_seq, n_heads, LANES), cache_k, cache_v, lam_init)
    x1_s = _outproj(xs, y_hg_s.reshape(n_seq, width), y_at_s.reshape(n_seq, width), mod_s,
                    pl.BlockSpec((n_seq, d), lambda i: (0, 2)), w_out16, n_seq)
    prev = state_conv[0]
    y_s, a_s = _ffn_sample(x1_s, mod_s, norm2, prev.reshape(n_seq, 2 * ff), w_up16, conv_w[0], conv_b,
                           w_down16, final_norm.reshape(1, d))
    conv_s = jnp.stack([prev[:, 1, :], a_s], axis=1)

    return (y_p.reshape(n_batch, seq, d), y_s.reshape(n_seq, 1, d),
            k_p.reshape(1, n_batch, seq, n_heads, LANES), v_p.reshape(1, n_batch, seq, n_heads, LANES),
            k_s.reshape(1, n_seq, 1, n_heads, LANES), v_s.reshape(1, n_seq, 1, n_heads, LANES),
            hgrn_p[None], hgrn_s[None], conv_p[None], conv_s[None])
```

```python
import functools
import math

import jax
import jax.numpy as jnp
from jax import lax
from jax.experimental import pallas as pl
from jax.experimental.pallas import tpu as pltpu

EPS = 1e-6
ROPE_THETA = 500000.0
LOG2_E = math.log2(math.e)
N_MOD = 6
LANES = 128
HGRN_CHUNK = 64
HGRN_SUB = 16
HALO = 16
VMEM_LIMIT_BYTES = 56 * 1024 * 1024

F32 = jnp.float32
BF16 = jnp.bfloat16


def _params(*sem):
    return pltpu.CompilerParams(dimension_semantics=sem, vmem_limit_bytes=VMEM_LIMIT_BYTES)


def _silu(x):
    return x / (1.0 + jnp.exp(-x))


def _sigmoid(x):
    return 1.0 / (1.0 + jnp.exp(-x))


def _rms(x):
    return x * lax.rsqrt(jnp.mean(x * x, axis=-1, keepdims=True) + EPS)


def _dot(a, b):
    return jnp.dot(a, b, preferred_element_type=F32)


def _dot_nt(a, b):
    return lax.dot_general(a, b, (((1,), (1,)), ((), ())), preferred_element_type=F32)


def _lam_value(lq1, lk1, lq2, lk2, lam_init):
    return (jnp.exp(jnp.sum(lq1 * lk1, axis=-1, keepdims=True))
            - jnp.exp(jnp.sum(lq2 * lk2, axis=-1, keepdims=True)) + lam_init)


def _lower_bound(layer_logits):
    mx = functools.reduce(jnp.maximum, layer_logits)
    es = [jnp.exp(r - mx) for r in layer_logits]
    return es[0] / functools.reduce(jnp.add, es)


def _adaln_kernel(c_ref, w_ref, b_ref, o_ref):
    s = _silu(c_ref[...]).astype(BF16)
    o_ref[...] = _dot(s, w_ref[...].astype(BF16)) + b_ref[...]


def _adaln(c_all, w_ada, b_ada, tn=512):
    rows, d = c_all.shape
    n = w_ada.shape[1]
    assert n % tn == 0
    return pl.pallas_call(
        _adaln_kernel,
        grid=(n // tn,),
        in_specs=[pl.BlockSpec((rows, d), lambda j: (0, 0)),
                  pl.BlockSpec((d, tn), lambda j: (0, j)),
                  pl.BlockSpec((1, tn), lambda j: (0, j))],
        out_specs=pl.BlockSpec((rows, tn), lambda j: (0, j)),
        out_shape=jax.ShapeDtypeStruct((rows, n), F32),
        compiler_params=_params("arbitrary"),
        name="adaln",
    )(c_all, w_ada, b_ada.reshape(1, n))


def _rope(blk, cos_t, sin_a, sin_b):
    return (blk * cos_t + pltpu.roll(blk, 8, axis=1) * sin_a
            + pltpu.roll(blk, LANES - 8, axis=1) * sin_b)


def _inproj_kernel(x_ref, sh_ref, sc_ref, g_ref, w_ref, cos_ref, sa_ref, sb_ref,
                   zq_ref, k_ref, v_ref, h_scr, *, n_heads):
    j = pl.program_id(1)

    @pl.when(j == 0)
    def _():
        h = _rms(x_ref[...]) * g_ref[...] * (1.0 + sc_ref[...]) + sh_ref[...]
        h_scr[...] = h.astype(BF16)

    acc = _dot(h_scr[...], w_ref[...])

    def roped(dst):
        for hh in range(n_heads):
            sl = slice(hh * LANES, (hh + 1) * LANES)
            dst[:, sl] = _rope(acc[:, sl], cos_ref[...], sa_ref[...], sb_ref[...])

    @pl.when(j < 4)
    def _():
        zq_ref[...] = acc

    @pl.when(j == 4)
    def _():
        roped(zq_ref)

    @pl.when(j == 5)
    def _():
        roped(k_ref)

    @pl.when(j == 6)
    def _():
        v_ref[...] = acc


def _inproj(x2, sh, sc, mod_spec, norm_g, w_in, tables, tab_spec, tm, n_heads):
    m, d = x2.shape
    tn = n_heads * LANES
    n_tiles = w_in.shape[1] // tn
    assert n_tiles == 7
    sh_spec, sc_spec = mod_spec
    return pl.pallas_call(
        functools.partial(_inproj_kernel, n_heads=n_heads),
        grid=(m // tm, n_tiles),
        in_specs=[pl.BlockSpec((tm, d), lambda i, j: (i, 0)),
                  sh_spec, sc_spec,
                  pl.BlockSpec((1, d), lambda i, j: (0, 0)),
                  pl.BlockSpec((d, tn), lambda i, j: (0, j)),
                  tab_spec, tab_spec, tab_spec],
        out_specs=[pl.BlockSpec((None, tm, tn), lambda i, j: (jnp.minimum(j, 4), i, 0)),
                   pl.BlockSpec((tm, tn), lambda i, j: (i, 0)),
                   pl.BlockSpec((tm, tn), lambda i, j: (i, 0))],
        out_shape=[jax.ShapeDtypeStruct((5, m, tn), F32),
                   jax.ShapeDtypeStruct((m, tn), F32),
                   jax.ShapeDtypeStruct((m, tn), F32)],
        scratch_shapes=[pltpu.VMEM((tm, d), BF16)],
        compiler_params=_params("arbitrary", "arbitrary"),
        name="inproj",
    )(x2, sh, sc, norm_g, w_in, *tables)


def _hgrn_chunk(q, hf, v, g, lb, gn, st, tri, row, row_in_sub):
    c_rows, sub = HGRN_CHUNK, HGRN_SUB
    f = lb + (1.0 - lb) * _sigmoid(hf)
    kk = 1.0 - f
    lf = jnp.log(f)
    hi = lf.astype(BF16)
    r1 = lf - hi.astype(F32)
    mid = r1.astype(BF16)
    lo = (r1 - mid.astype(F32)).astype(BF16)
    b = _dot(tri, hi) + _dot(tri, mid) + _dot(tri, lo)
    b_last = b[c_rows - 1:c_rows, :]

    o = _dot_nt((q * jnp.exp(b)).astype(BF16), st.astype(BF16))
    kd = kk * jnp.exp(b_last - b)
    st_new = st * jnp.exp(b_last) + _dot(v.T.astype(BF16), kd.astype(BF16))

    v16 = v.astype(BF16)
    for jb in range(c_rows // sub - 1):
        e = (jb + 1) * sub
        r_ref = b[e - 1:e, :]
        qj = q * jnp.exp(jnp.minimum(b - r_ref, 0.0))
        kj = kk[jb * sub:e, :] * jnp.exp(r_ref - b[jb * sub:e, :])
        att = _dot_nt(qj.astype(BF16), kj.astype(BF16))
        att = jnp.where(row >= e, att, 0.0)
        o = o + _dot(att.astype(BF16), v16[jb * sub:e, :])
    for dlt in range(sub):
        if dlt == 0:
            w = q * kk
            vs = v
        else:
            ks = pltpu.roll(kk, dlt, axis=0)
            bs = pltpu.roll(b, dlt, axis=0)
            vs = pltpu.roll(v, dlt, axis=0)
            w = q * ks * jnp.exp(b - bs)
        a = jnp.sum(w, axis=-1, keepdims=True)
        a = jnp.where(row_in_sub >= dlt, a, 0.0)
        o = o + a * vs
    return _rms(o) * gn * _silu(g), st_new


def _hgrn_prompt_kernel(q_ref, f_ref, v_ref, g_ref, lbl_ref, gn_ref, y_ref, sfin_ref, st_scr,
                        *, rows, group):
    t = pl.program_id(2)
    c_rows = HGRN_CHUNK

    @pl.when(t == 0)
    def _():
        st_scr[...] = jnp.zeros_like(st_scr)

    lb = _lower_bound([lbl_ref[l:l + 1, :] for l in range(lbl_ref.shape[0])])
    ri = lax.broadcasted_iota(jnp.int32, (c_rows, c_rows), 0)
    ci = lax.broadcasted_iota(jnp.int32, (c_rows, c_rows), 1)
    tri = jnp.where(ri >= ci, 1.0, 0.0).astype(BF16)
    row = lax.broadcasted_iota(jnp.int32, (c_rows, 1), 0)
    row_in_sub = row % HGRN_SUB

    def chunk(c, carry):
        r0 = pl.multiple_of(c * c_rows, c_rows)
        for gi in range(group):
            sl = slice(gi * LANES, (gi + 1) * LANES)
            y, st_new = _hgrn_chunk(q_ref[pl.ds(r0, c_rows), sl], f_ref[pl.ds(r0, c_rows), sl],
                                    v_ref[pl.ds(r0, c_rows), sl], g_ref[pl.ds(r0, c_rows), sl],
                                    lb[:, sl], gn_ref[...], st_scr[gi], tri, row, row_in_sub)
            st_scr[gi] = st_new
            y_ref[pl.ds(r0, c_rows), sl] = y
        return carry

    lax.fori_loop(0, rows // c_rows, chunk, 0)

    @pl.when(t == pl.num_programs(2) - 1)
    def _():
        for gi in range(group):
            sfin_ref[gi] = st_scr[gi].T


def _hgrn_prompt(zq, lb_logits, hg_norm, n_batch, seq, n_heads, rows=512, group=4):
    m = zq.shape[1]
    nt = seq // rows
    group = math.gcd(group, n_heads)
    gw = group * LANES

    def zspec(part):
        return pl.BlockSpec((None, rows, gw), lambda b, h, t: (part, b * nt + t, h))

    return pl.pallas_call(
        functools.partial(_hgrn_prompt_kernel, rows=rows, group=group),
        grid=(n_batch, n_heads // group, nt),
        in_specs=[zspec(0), zspec(1), zspec(2), zspec(3),
                  pl.BlockSpec((lb_logits.shape[0], gw), lambda b, h, t: (0, h)),
                  pl.BlockSpec((1, LANES), lambda b, h, t: (0, 0))],
        out_specs=[pl.BlockSpec((rows, gw), lambda b, h, t: (b * nt + t, h)),
                   pl.BlockSpec((None, group, LANES, LANES), lambda b, h, t: (b, h, 0, 0))],
        out_shape=[jax.ShapeDtypeStruct((m, n_heads * LANES), F32),
                   jax.ShapeDtypeStruct((n_batch, n_heads, LANES, LANES), F32)],
        scratch_shapes=[pltpu.VMEM((group, LANES, LANES), F32)],
        compiler_params=_params("arbitrary", "arbitrary", "arbitrary"),
        name="hgrn_prompt",
    )(zq, zq, zq, zq, lb_logits, hg_norm)


def _hgrn_sample_kernel(q_ref, f_ref, v_ref, g_ref, lbl_ref, gn_ref, s_ref, y_ref, so_ref,
                        *, n_heads):
    lb = _lower_bound([lbl_ref[l] for l in range(lbl_ref.shape[0])])
    q = q_ref[...]
    v = v_ref[...]
    f = lb + (1.0 - lb) * _sigmoid(f_ref[...])
    kk = 1.0 - f
    pad = jnp.zeros((LANES - 3 * n_heads, LANES), F32)
    cols = jnp.concatenate([f, kk, q, pad], axis=0).T
    o_rows = []
    for h in range(n_heads):
        f_col = cols[:, h:h + 1]
        k_col = cols[:, n_heads + h:n_heads + h + 1]
        q_col = cols[:, 2 * n_heads + h:2 * n_heads + h + 1]
        s_new = f_col * s_ref[h] + k_col * v[h:h + 1, :]
        so_ref[h] = s_new
        o_rows.append(jnp.sum(q_col * s_new, axis=0, keepdims=True))
    o = jnp.concatenate(o_rows, axis=0)
    y_ref[...] = _rms(o) * gn_ref[...] * _silu(g_ref[...])


def _hgrn_sample(zq4, lb_logits3, hg_norm, state):
    n_seq, n_heads = state.shape[0], state.shape[1]

    def zspec(part):
        return pl.BlockSpec((None, None, n_heads, LANES), lambda b: (part, b, 0, 0))

    return pl.pallas_call(
        functools.partial(_hgrn_sample_kernel, n_heads=n_heads),
        grid=(n_seq,),
        in_specs=[zspec(0), zspec(1), zspec(2), zspec(3),
                  pl.BlockSpec(lb_logits3.shape, lambda b: (0, 0, 0)),
                  pl.BlockSpec((1, LANES), lambda b: (0, 0)),
                  pl.BlockSpec((None, n_heads, LANES, LANES), lambda b: (b, 0, 0, 0))],
        out_specs=[pl.BlockSpec((None, n_heads, LANES), lambda b: (b, 0, 0)),
                   pl.BlockSpec((None, n_heads, LANES, LANES), lambda b: (b, 0, 0, 0))],
        out_shape=[jax.ShapeDtypeStruct((n_seq, n_heads, LANES), F32),
                   jax.ShapeDtypeStruct(state.shape, F32)],
        compiler_params=_params("arbitrary"),
        name="hgrn_sample",
    )(zq4, zq4, zq4, zq4, lb_logits3, hg_norm, state)


def _attn_prompt_kernel(lq1_ref, lk1_ref, lq2_ref, lk2_ref, an_ref, q_ref, k_ref, v_ref, y_ref,
                        k16_scr, vt_scr, qm_scr, m_scr, l_scr, acc_scr, *, lam_init, tq, tk, qk_dim):
    qi = pl.program_id(2)
    n_kv = vt_scr.shape[0]
    per_q = tq // tk

    @pl.when(qi == 0)
    def _():
        k16_scr[...] = k_ref[...].astype(BF16)
        for t in range(n_kv):
            vt_scr[t] = v_ref[t * tk:(t + 1) * tk, :].T.astype(BF16)

    q = q_ref[...] * (qk_dim ** -0.5 * LOG2_E)
    lane = lax.broadcasted_iota(jnp.int32, (1, LANES), 1)
    qm_scr[:tq, :] = jnp.where(lane < qk_dim, q, 0.0).astype(BF16)
    qm_scr[tq:, :] = jnp.where(lane >= qk_dim, q, 0.0).astype(BF16)
    m_scr[...] = jnp.full_like(m_scr, -jnp.inf)
    l_scr[...] = jnp.zeros_like(l_scr)
    acc_scr[...] = jnp.zeros_like(acc_scr)

    def kv_tile(kj, diagonal):
        r0 = pl.multiple_of(kj * tk, tk)
        st = _dot_nt(k16_scr[pl.ds(r0, tk), :], qm_scr[...])
        if diagonal is not None:
            key = lax.broadcasted_iota(jnp.int32, (tk, 2 * tq), 0) + diagonal * tk
            qry = lax.broadcasted_iota(jnp.int32, (tk, 2 * tq), 1) % tq
            st = jnp.where(key <= qry, st, -jnp.inf)
        m_old = m_scr[...]
        m_new = jnp.maximum(m_old, jnp.max(st, axis=0, keepdims=True))
        p = jnp.exp2(st - m_new)
        corr = jnp.exp2(m_old - m_new)
        l_scr[...] = l_scr[...] * corr + jnp.sum(p.reshape(tk // 8, 8, 2 * tq), axis=0)
        acc_scr[...] = acc_scr[...] * corr + _dot(vt_scr[kj], p.astype(BF16))
        m_scr[...] = m_new

    def body(kj, carry):
        kv_tile(kj, None)
        return carry

    lax.fori_loop(0, qi * per_q, body, 0)
    for dg in range(per_q):
        kv_tile(qi * per_q + dg, dg)

    lam = _lam_value(lq1_ref[...], lk1_ref[...], lq2_ref[...], lk2_ref[...], lam_init)
    o = acc_scr[...] / jnp.sum(l_scr[...], axis=0, keepdims=True)
    o = (o[:, :tq] - lam * o[:, tq:]).T
    y_ref[...] = _rms(o) * an_ref[...] * (1.0 - lam_init)


def _attn_prompt(lam_params, at_norm, zq, k, v, n_batch, seq, n_heads, lam_init, tq=512, tk=512):
    m = k.shape[0]
    nt = seq // tq
    qk_dim = lam_params[0].shape[1]
    small = [pl.BlockSpec(p.shape, lambda b, h, i: (0, 0)) for p in lam_params]
    kv_spec = pl.BlockSpec((seq, LANES), lambda b, h, i: (b, h))

    return pl.pallas_call(
        functools.partial(_attn_prompt_kernel, lam_init=lam_init, tq=tq, tk=tk, qk_dim=qk_dim),
        grid=(n_batch, n_heads, nt),
        in_specs=small + [pl.BlockSpec((1, LANES), lambda b, h, i: (0, 0)),
                          pl.BlockSpec((None, tq, LANES), lambda b, h, i: (4, b * nt + i, h)),
                          kv_spec, kv_spec],
        out_specs=pl.BlockSpec((tq, LANES), lambda b, h, i: (b * nt + i, h)),
        out_shape=jax.ShapeDtypeStruct((m, n_heads * LANES), F32),
        scratch_shapes=[pltpu.VMEM((seq, LANES), BF16), pltpu.VMEM((seq // tk, LANES, tk), BF16),
                        pltpu.VMEM((2 * tq, LANES), BF16), pltpu.VMEM((1, 2 * tq), F32),
                        pltpu.VMEM((8, 2 * tq), F32), pltpu.VMEM((LANES, 2 * tq), F32)],
        compiler_params=_params("arbitrary", "arbitrary", "arbitrary"),
        name="attn_prompt",
    )(*lam_params, at_norm, zq, k, v)


def _attn_sample_kernel(pt_ref, lq1_ref, lk1_ref, lq2_ref, lk2_ref, an_ref, q_ref, kn_ref, vn_ref,
                        *rest, lam_init, n_heads, qk_dim, pages):
    k_refs = rest[:pages]
    v_refs = rest[pages:2 * pages]
    y_ref, m_scr, l_scr, acc_scr = rest[2 * pages:]
    j = pl.program_id(1)
    page_rows = k_refs[0].shape[0] * n_heads

    q = q_ref[...] * (qk_dim ** -0.5)
    lane = lax.broadcasted_iota(jnp.int32, (1, LANES), 1)
    qm = jnp.concatenate([jnp.where(lane < qk_dim, q, 0.0), jnp.where(lane >= qk_dim, q, 0.0)], axis=0)

    @pl.when(j == 0)
    def _():
        kn = kn_ref[...]
        vn = vn_ref[...]
        m_scr[...] = jnp.sum(qm * jnp.concatenate([kn, kn], axis=0), axis=-1, keepdims=True)
        l_scr[...] = jnp.ones_like(l_scr)
        acc_scr[...] = jnp.concatenate([vn, vn], axis=0)

    qm16 = qm.astype(BF16)
    own_head = (lax.broadcasted_iota(jnp.int32, (2 * n_heads, page_rows), 1) % n_heads
                == lax.broadcasted_iota(jnp.int32, (2 * n_heads, page_rows), 0) % n_heads)
    scores = [jnp.where(own_head, _dot_nt(qm16, k_refs[p][...].reshape(page_rows, LANES).astype(BF16)),
                        -jnp.inf) for p in range(pages)]
    m_old = m_scr[...]
    m_new = functools.reduce(jnp.maximum, [jnp.max(s, axis=-1, keepdims=True) for s in scores] + [m_old])
    corr = jnp.exp(m_old - m_new)
    l_new = l_scr[...] * corr
    acc = acc_scr[...] * corr
    for p in range(pages):
        pr = jnp.exp(scores[p] - m_new)
        l_new = l_new + jnp.sum(pr, axis=-1, keepdims=True)
        acc = acc + _dot(pr.astype(BF16), v_refs[p][...].reshape(page_rows, LANES).astype(BF16))
    l_scr[...] = l_new
    acc_scr[...] = acc
    m_scr[...] = m_new

    @pl.when(j == pl.num_programs(1) - 1)
    def _():
        lam = _lam_value(lq1_ref[...], lk1_ref[...], lq2_ref[...], lk2_ref[...], lam_init)
        o = acc_scr[...] / l_scr[...]
        w = o[:n_heads] - lam * o[n_heads:]
        y_ref[...] = _rms(w) * an_ref[...] * (1.0 - lam_init)


def _attn_sample(page_table, lam_params, at_norm, q3, kn3, vn3, cache_k, cache_v, lam_init, pages=8):
    n_seq, n_pages = page_table.shape
    _, _, page_size, n_heads, _ = cache_k.shape
    qk_dim = lam_params[0].shape[1]
    assert n_pages % pages == 0
    small = [pl.BlockSpec(p.shape, lambda b, j, pt: (0, 0)) for p in lam_params]
    row_spec = pl.BlockSpec((None, n_heads, LANES), lambda b, j, pt: (b, 0, 0))

    def page_spec(p):
        return pl.BlockSpec(
            (None, None, page_size, n_heads, LANES),
            lambda b, j, pt: (0, pt[b * n_pages + j * pages + p], 0, 0, 0))

    grid_spec = pltpu.PrefetchScalarGridSpec(
        num_scalar_prefetch=1,
        grid=(n_seq, n_pages // pages),
        in_specs=(small + [pl.BlockSpec((1, LANES), lambda b, j, pt: (0, 0)), row_spec, row_spec, row_spec]
                  + [page_spec(p) for p in range(pages)] + [page_spec(p) for p in range(pages)]),
        out_specs=row_spec,
        scratch_shapes=[pltpu.VMEM((2 * n_heads, 1), F32), pltpu.VMEM((2 * n_heads, 1), F32),
                        pltpu.VMEM((2 * n_heads, LANES), F32)],
    )
    return pl.pallas_call(
        functools.partial(_attn_sample_kernel, lam_init=lam_init, n_heads=n_heads, qk_dim=qk_dim,
                          pages=pages),
        grid_spec=grid_spec,
        out_shape=jax.ShapeDtypeStruct((n_seq, n_heads, LANES), F32),
        compiler_params=_params("arbitrary", "arbitrary"),
        name="attn_sample",
    )(page_table.reshape(-1), *lam_params, at_norm, q3, kn3, vn3,
      *([cache_k] * pages), *([cache_v] * pages))


def _outproj_kernel(x_ref, yh_ref, ya_ref, g1_ref, w_ref, o_ref):
    hw = yh_ref.shape[1]
    acc = _dot(yh_ref[...].astype(BF16), w_ref[:hw, :]) + _dot(ya_ref[...].astype(BF16), w_ref[hw:, :])
    o_ref[...] = x_ref[...] + g1_ref[...] * acc


def _outproj(x2, y_hg, y_at, g1, g1_spec, w_out, tm):
    m, d = x2.shape
    return pl.pallas_call(
        _outproj_kernel,
        grid=(m // tm,),
        in_specs=[pl.BlockSpec((tm, d), lambda i: (i, 0)),
                  pl.BlockSpec((tm, y_hg.shape[1]), lambda i: (i, 0)),
                  pl.BlockSpec((tm, y_at.shape[1]), lambda i: (i, 0)),
                  g1_spec,
                  pl.BlockSpec(w_out.shape, lambda i: (0, 0))],
        out_specs=pl.BlockSpec((tm, d), lambda i: (i, 0)),
        out_shape=jax.ShapeDtypeStruct((m, d), F32),
        compiler_params=_params("arbitrary"),
        name="outproj",
    )(x2, y_hg, y_at, g1, w_out)


def _ffn_prompt_kernel(x_ref, halo_ref, sh_ref, sc_ref, g2_ref, n2_ref, wa_ref, wu_ref, cw_ref, cb_ref,
                       wd_ref, fn_ref, y_ref, tail_ref, h_scr, a_scr, acc_scr, *, tm, tiles_per_seq):
    i = pl.program_id(0)
    f = pl.program_id(1)

    def modulated(x):
        return (_rms(x) * n2_ref[...] * (1.0 + sc_ref[...]) + sh_ref[...]).astype(BF16)

    @pl.when(f == 0)
    def _():
        h_scr[:HALO, :] = modulated(halo_ref[...])
        h_scr[HALO:, :] = modulated(x_ref[...])
        acc_scr[...] = jnp.zeros_like(acc_scr)

    a_scr[...] = _dot(h_scr[...], wa_ref[...])

    @pl.when(i % tiles_per_seq == 0)
    def _():
        a_scr[:HALO, :] = jnp.zeros((HALO, a_scr.shape[1]), F32)

    u = _dot(h_scr[HALO:, :], wu_ref[...])
    conv = (cb_ref[...] + a_scr[HALO - 2:HALO - 2 + tm, :] * cw_ref[0:1, :]
            + a_scr[HALO - 1:HALO - 1 + tm, :] * cw_ref[1:2, :]
            + a_scr[HALO:, :] * cw_ref[2:3, :])
    acc_scr[...] += _dot((_silu(conv) * u).astype(BF16), wd_ref[...])
    tail_ref[...] = a_scr[HALO + tm - 2:, :]

    @pl.when(f == pl.num_programs(1) - 1)
    def _():
        y_ref[...] = _rms(x_ref[...] + g2_ref[...] * acc_scr[...]) * fn_ref[...]


def _ffn_prompt(x1, mod_p, norm2, w_up, conv_w, conv_b, w_down, final_norm, n_batch, seq, tm=512, tf=512):
    m, d = x1.shape
    ff = w_down.shape[0]
    nf = ff // tf
    tps = seq // tm
    assert conv_w.shape[0] == 3

    def mspec(part):
        return pl.BlockSpec((None, 1, d), lambda i, f: (i // tps, 0, part))

    y, tails = pl.pallas_call(
        functools.partial(_ffn_prompt_kernel, tm=tm, tiles_per_seq=tps),
        grid=(m // tm, nf),
        in_specs=[pl.BlockSpec((tm, d), lambda i, f: (i, 0)),
                  pl.BlockSpec((HALO, d), lambda i, f: (jnp.maximum(i * (tm // HALO) - 1, 0), 0)),
                  mspec(3), mspec(4), mspec(5),
                  pl.BlockSpec((1, d), lambda i, f: (0, 0)),
                  pl.BlockSpec((d, tf), lambda i, f: (0, f)),
                  pl.BlockSpec((d, tf), lambda i, f: (0, nf + f)),
                  pl.BlockSpec((3, tf), lambda i, f: (0, f)),
                  pl.BlockSpec((1, tf), lambda i, f: (0, f)),
                  pl.BlockSpec((tf, d), lambda i, f: (f, 0)),
                  pl.BlockSpec((1, d), lambda i, f: (0, 0))],
        out_specs=[pl.BlockSpec((tm, d), lambda i, f: (i, 0)),
                   pl.BlockSpec((None, 2, tf), lambda i, f: (i, 0, f))],
        out_shape=[jax.ShapeDtypeStruct((m, d), F32),
                   jax.ShapeDtypeStruct((m // tm, 2, ff), F32)],
        scratch_shapes=[pltpu.VMEM((HALO + tm, d), BF16), pltpu.VMEM((HALO + tm, tf), F32),
                        pltpu.VMEM((tm, d), F32)],
        compiler_params=_params("arbitrary", "arbitrary"),
        name="ffn_prompt",
    )(x1, x1, mod_p, mod_p, mod_p, norm2, w_up, w_up, conv_w, conv_b, w_down, final_norm)
    return y, tails[tps - 1::tps]


def _ffn_sample_kernel(x_ref, sh_ref, sc_ref, g2_ref, n2_ref, p0_ref, p1_ref, wa_ref, wu_ref, cw_ref,
                       cb_ref, wd_ref, fn_ref, y_ref, a_ref, h_scr, acc_scr):
    f = pl.program_id(0)

    @pl.when(f == 0)
    def _():
        h = _rms(x_ref[...]) * n2_ref[...] * (1.0 + sc_ref[...]) + sh_ref[...]
        h_scr[...] = h.astype(BF16)
        acc_scr[...] = jnp.zeros_like(acc_scr)

    a = _dot(h_scr[...], wa_ref[...])
    u = _dot(h_scr[...], wu_ref[...])
    conv = (cb_ref[...] + p0_ref[...] * cw_ref[0:1, :] + p1_ref[...] * cw_ref[1:2, :]
            + a * cw_ref[2:3, :])
    acc_scr[...] += _dot((_silu(conv) * u).astype(BF16), wd_ref[...])
    a_ref[...] = a

    @pl.when(f == pl.num_programs(0) - 1)
    def _():
        y_ref[...] = _rms(x_ref[...] + g2_ref[...] * acc_scr[...]) * fn_ref[...]


def _ffn_sample(x1, mod_s, norm2, prev2, w_up, conv_w, conv_b, w_down, final_norm, tf=512):
    m, d = x1.shape
    ff = w_down.shape[0]
    nf = ff // tf

    def mspec(part):
        return pl.BlockSpec((m, d), lambda f: (0, part))

    return pl.pallas_call(
        _ffn_sample_kernel,
        grid=(nf,),
        in_specs=[pl.BlockSpec((m, d), lambda f: (0, 0)),
                  mspec(3), mspec(4), mspec(5),
                  pl.BlockSpec((1, d), lambda f: (0, 0)),
                  pl.BlockSpec((m, tf), lambda f: (0, f)),
                  pl.BlockSpec((m, tf), lambda f: (0, nf + f)),
                  pl.BlockSpec((d, tf), lambda f: (0, f)),
                  pl.BlockSpec((d, tf), lambda f: (0, nf + f)),
                  pl.BlockSpec((3, tf), lambda f: (0, f)),
                  pl.BlockSpec((1, tf), lambda f: (0, f)),
                  pl.BlockSpec((tf, d), lambda f: (f, 0)),
                  pl.BlockSpec((1, d), lambda f: (0, 0))],
        out_specs=[pl.BlockSpec((m, d), lambda f: (0, 0)),
                   pl.BlockSpec((m, tf), lambda f: (0, f))],
        out_shape=[jax.ShapeDtypeStruct((m, d), F32),
                   jax.ShapeDtypeStruct((m, ff), F32)],
        scratch_shapes=[pltpu.VMEM((m, d), BF16), pltpu.VMEM((m, d), F32)],
        compiler_params=_params("arbitrary"),
        name="ffn_sample",
    )(x1, mod_s, mod_s, mod_s, norm2, prev2, prev2, w_up, w_up, conv_w, conv_b, w_down, final_norm)


def _rope_tables(pos, qk_dim):
    rot = qk_dim // 4
    half = rot // 2
    inv_freq = ROPE_THETA ** (-jnp.arange(half, dtype=F32) / half)
    ang = pos.astype(F32)[:, None] * inv_freq[None, :]
    cos, sin = jnp.cos(ang), jnp.sin(ang)
    ones = jnp.ones((pos.shape[0], qk_dim - rot), F32)
    zeros = jnp.zeros((pos.shape[0], qk_dim - rot), F32)
    zh = jnp.zeros_like(sin)
    cos_t = jnp.concatenate([cos, cos, ones] * 2, axis=1)
    sin_a = jnp.concatenate([zh, sin, zeros] * 2, axis=1)
    sin_b = jnp.concatenate([-sin, zh, zeros] * 2, axis=1)
    return cos_t, sin_a, sin_b


def kernel(x_prompt, x_sample, c_prompt, c_sample, cache_k, cache_v, state_hgrn, state_conv, page_table,
           w_ada, b_ada, norm1, norm2, w_in, hg_lb_logits, hg_norm, lam_q1, lam_k1, lam_q2, lam_k2,
           at_norm, w_out, w_up, conv_w, conv_b, w_down, final_norm):
    n_batch, seq, d = x_prompt.shape
    n_seq = x_sample.shape[0]
    depth = w_in.shape[0]
    assert depth == 1 and x_sample.shape[1] == 1
    n_heads = state_hgrn.shape[2]
    assert cache_k.shape[3] == n_heads and cache_k.shape[4] == LANES and state_hgrn.shape[3] == LANES
    qk_dim = lam_q1.shape[1]
    past_len = page_table.shape[1] * cache_k.shape[2]
    lam_init = 0.8 - 0.6 * math.exp(-0.3 * 0)
    ff = w_down.shape[1]
    width = n_heads * LANES
    lam_params = [lam_q1, lam_k1, lam_q2, lam_k2]

    w_in16 = w_in[0].astype(BF16)
    w_out16 = w_out[0].astype(BF16)
    w_up16 = w_up[0].astype(BF16)
    w_down16 = w_down[0].astype(BF16)

    mod = _adaln(jnp.concatenate([c_prompt, c_sample], axis=0), w_ada[0], b_ada[0])
    mod_p = mod[:n_batch].reshape(n_batch, 1, N_MOD * d)
    mod_s = mod[n_batch:]

    tm = 512
    tps = seq // tm
    xp = x_prompt.reshape(n_batch * seq, d)
    tables_p = _rope_tables(jnp.arange(seq, dtype=jnp.int32), qk_dim)

    def pspec(part):
        return pl.BlockSpec((None, 1, d), lambda i, j: (i // tps, 0, part))

    zq, k_p, v_p = _inproj(xp, mod_p, mod_p, (pspec(0), pspec(1)), norm1, w_in16, tables_p,
                           pl.BlockSpec((tm, LANES), lambda i, j: (i % tps, 0)), tm, n_heads)
    y_hg, hgrn_p = _hgrn_prompt(zq, hg_lb_logits, hg_norm, n_batch, seq, n_heads)
    y_at = _attn_prompt(lam_params, at_norm, zq, k_p, v_p, n_batch, seq, n_heads, lam_init)
    x1 = _outproj(xp, y_hg, y_at, mod_p, pl.BlockSpec((None, 1, d), lambda i: (i // tps, 0, 2)),
                  w_out16, tm)
    y_p, conv_p = _ffn_prompt(x1, mod_p, norm2, w_up16, conv_w[0], conv_b, w_down16,
                              final_norm.reshape(1, d), n_batch, seq)

    xs = x_sample.reshape(n_seq, d)
    tables_s = _rope_tables(jnp.full((n_seq,), past_len, jnp.int32), qk_dim)

    def sspec(part):
        return pl.BlockSpec((n_seq, d), lambda i, j: (0, part))

    zq_s, k_s, v_s = _inproj(xs, mod_s, mod_s, (sspec(0), sspec(1)), norm1, w_in16, tables_s,
                             pl.BlockSpec((n_seq, LANES), lambda i, j: (0, 0)), n_seq, n_heads)
    zq_s4 = zq_s.reshape(5, n_seq, n_heads, LANES)
    y_hg_s, hgrn_s = _hgrn_sample(zq_s4, hg_lb_logits.reshape(-1, n_heads, LANES), hg_norm, state_hgrn[0])
    y_at_s = _attn_sample(page_table, lam_params, at_norm, zq_s4[4], k_s.reshape(n_seq, n_heads, LANES),
                          v_s.reshape(n_seq, n_heads, LANES), cache_k, cache_v, lam_init)
    x1_s = _outproj(xs, y_hg_s.reshape(n_seq, width), y_at_s.reshape(n_seq, width), mod_s,
                    pl.BlockSpec((n_seq, d), lambda i: (0, 2)), w_out16, n_seq)
    prev = state_conv[0]
    y_s, a_s = _ffn_sample(x1_s, mod_s, norm2, prev.reshape(n_seq, 2 * ff), w_up16, conv_w[0], conv_b,
                           w_down16, final_norm.reshape(1, d))
    conv_s = jnp.stack([prev[:, 1, :], a_s], axis=1)

    return (y_p.reshape(n_batch, seq, d), y_s.reshape(n_seq, 1, d),
            k_p.reshape(1, n_batch, seq, n_heads, LANES), v_p.reshape(1, n_batch, seq, n_heads, LANES),
            k_s.reshape(1, n_seq, 1, n_heads, LANES), v_s.reshape(1, n_seq, 1, n_heads, LANES),
            hgrn_p[None], hgrn_s[None], conv_p[None], conv_s[None])
```

```python
import functools
import math

import jax
import jax.numpy as jnp
from jax import lax
from jax.experimental import pallas as pl
from jax.experimental.pallas import tpu as pltpu

EPS = 1e-6
ROPE_THETA = 500000.0
LOG2_E = math.log2(math.e)
N_MOD = 6
LANES = 128
HGRN_CHUNK = 64
HGRN_SUB = 16
HGRN_TILE = 8
HALO = 16
VMEM_LIMIT_BYTES = 56 * 1024 * 1024

F32 = jnp.float32
BF16 = jnp.bfloat16


def _params(*sem):
    return pltpu.CompilerParams(dimension_semantics=sem, vmem_limit_bytes=VMEM_LIMIT_BYTES)


def _silu(x):
    return x / (1.0 + jnp.exp(-x))


def _sigmoid(x):
    return 1.0 / (1.0 + jnp.exp(-x))


def _rms(x):
    return x * lax.rsqrt(jnp.mean(x * x, axis=-1, keepdims=True) + EPS)


def _dot(a, b):
    return jnp.dot(a, b, preferred_element_type=F32)


def _dot_nt(a, b):
    return lax.dot_general(a, b, (((1,), (1,)), ((), ())), preferred_element_type=F32)


def _lam_value(lq1, lk1, lq2, lk2, lam_init):
    return (jnp.exp(jnp.sum(lq1 * lk1, axis=-1, keepdims=True))
            - jnp.exp(jnp.sum(lq2 * lk2, axis=-1, keepdims=True)) + lam_init)


def _lower_bound(layer_logits):
    mx = functools.reduce(jnp.maximum, layer_logits)
    es = [jnp.exp(r - mx) for r in layer_logits]
    return es[0] / functools.reduce(jnp.add, es)


def _adaln_kernel(c_ref, w_ref, b_ref, o_ref):
    s = _silu(c_ref[...]).astype(BF16)
    o_ref[...] = _dot(s, w_ref[...].astype(BF16)) + b_ref[...]


def _adaln(c_all, w_ada, b_ada, tn=512):
    rows, d = c_all.shape
    n = w_ada.shape[1]
    assert n % tn == 0
    return pl.pallas_call(
        _adaln_kernel,
        grid=(n // tn,),
        in_specs=[pl.BlockSpec((rows, d), lambda j: (0, 0)),
                  pl.BlockSpec((d, tn), lambda j: (0, j)),
                  pl.BlockSpec((1, tn), lambda j: (0, j))],
        out_specs=pl.BlockSpec((rows, tn), lambda j: (0, j)),
        out_shape=jax.ShapeDtypeStruct((rows, n), F32),
        compiler_params=_params("arbitrary"),
        name="adaln",
    )(c_all, w_ada, b_ada.reshape(1, n))


def _rope(blk, cos_t, sin_a, sin_b):
    return (blk * cos_t + pltpu.roll(blk, 8, axis=1) * sin_a
            + pltpu.roll(blk, LANES - 8, axis=1) * sin_b)


def _inproj_kernel(x_ref, sh_ref, sc_ref, g_ref, w_ref, cos_ref, sa_ref, sb_ref,
                   zq_ref, k_ref, v_ref, h_scr, *, split, sub_blocks):
    part = pl.program_id(1) // split

    @pl.when(pl.program_id(1) == 0)
    def _():
        h = _rms(x_ref[...]) * g_ref[...] * (1.0 + sc_ref[...]) + sh_ref[...]
        h_scr[...] = h.astype(BF16)

    tm, tn = zq_ref.shape
    rows = tm // sub_blocks

    def project(dst, rotary):
        for r in range(sub_blocks):
            rs = slice(r * rows, (r + 1) * rows)
            acc = _dot(h_scr[rs, :], w_ref[...])
            if rotary:
                for hh in range(tn // LANES):
                    sl = slice(hh * LANES, (hh + 1) * LANES)
                    dst[rs, sl] = _rope(acc[:, sl], cos_ref[rs, :], sa_ref[rs, :], sb_ref[rs, :])
            else:
                dst[rs, :] = acc

    @pl.when(part < 4)
    def _():
        project(zq_ref, False)

    @pl.when(part == 4)
    def _():
        project(zq_ref, True)

    @pl.when(part == 5)
    def _():
        project(k_ref, True)

    @pl.when(part == 6)
    def _():
        project(v_ref, False)


def _inproj(x2, sh, sc, mod_spec, norm_g, w_in, tables, tab_spec, tm, n_heads, split=2):
    m, d = x2.shape
    width = n_heads * LANES
    assert w_in.shape[1] == 7 * width and n_heads % split == 0
    tn = width // split
    sh_spec, sc_spec = mod_spec

    def zq_map(i, j):
        jz = jnp.minimum(j, 5 * split - 1)
        return (jz // split, i, jz % split)

    return pl.pallas_call(
        functools.partial(_inproj_kernel, split=split, sub_blocks=max(1, tm // 256)),
        grid=(m // tm, 7 * split),
        in_specs=[pl.BlockSpec((tm, d), lambda i, j: (i, 0)),
                  sh_spec, sc_spec,
                  pl.BlockSpec((1, d), lambda i, j: (0, 0)),
                  pl.BlockSpec((d, tn), lambda i, j: (0, j)),
                  tab_spec, tab_spec, tab_spec],
        out_specs=[pl.BlockSpec((None, tm, tn), zq_map),
                   pl.BlockSpec((tm, tn), lambda i, j: (i, jnp.clip(j - 5 * split, 0, split - 1))),
                   pl.BlockSpec((tm, tn), lambda i, j: (i, jnp.clip(j - 6 * split, 0, split - 1)))],
        out_shape=[jax.ShapeDtypeStruct((5, m, width), F32),
                   jax.ShapeDtypeStruct((m, width), F32),
                   jax.ShapeDtypeStruct((m, width), F32)],
        scratch_shapes=[pltpu.VMEM((tm, d), BF16)],
        compiler_params=_params("arbitrary", "arbitrary"),
        name="inproj",
    )(x2, sh, sc, norm_g, w_in, *tables)


def _roll_in_tiles(x, shift):
    if shift == 0:
        return x
    tiles = [pltpu.roll(x[r:r + HGRN_TILE, :], shift, axis=0) for r in range(0, x.shape[0], HGRN_TILE)]
    return jnp.concatenate(tiles, axis=0)


def _hgrn_masks():
    c_rows = HGRN_CHUNK
    ri = lax.broadcasted_iota(jnp.int32, (c_rows, c_rows), 0)
    ci = lax.broadcasted_iota(jnp.int32, (c_rows, c_rows), 1)
    tri = jnp.where(ri >= ci, 1.0, 0.0).astype(BF16)
    same_sub = ri // HGRN_SUB == ci // HGRN_SUB
    same_tile = ri // HGRN_TILE == ci // HGRN_TILE
    diag = [same_tile & (ci == ri - dlt) for dlt in range(HGRN_TILE)]
    row = lax.broadcasted_iota(jnp.int32, (c_rows, 1), 0)
    return tri, same_sub, diag, row


def _hgrn_chunk(q, hf, v, g, lb, gn, st, masks):
    c_rows, sub, tile = HGRN_CHUNK, HGRN_SUB, HGRN_TILE
    tri, same_sub, diag, row = masks
    f = lb + (1.0 - lb) * _sigmoid(hf)
    kk = 1.0 - f
    lf = jnp.log(f)
    hi = lf.astype(BF16)
    r1 = lf - hi.astype(F32)
    mid = r1.astype(BF16)
    lo = (r1 - mid.astype(F32)).astype(BF16)
    b = _dot(tri, hi) + _dot(tri, mid) + _dot(tri, lo)
    b_last = b[c_rows - 1:c_rows, :]

    o = _dot_nt((q * jnp.exp(b)).astype(BF16), st.astype(BF16))
    kd = kk * jnp.exp(b_last - b)
    st_new = st * jnp.exp(b_last) + _dot(v.T.astype(BF16), kd.astype(BF16))

    v16 = v.astype(BF16)
    for jb in range(c_rows // sub - 1):
        e = (jb + 1) * sub
        r_ref = b[e - 1:e, :]
        qj = q * jnp.exp(jnp.minimum(b - r_ref, 0.0))
        kj = kk[jb * sub:e, :] * jnp.exp(r_ref - b[jb * sub:e, :])
        att = _dot_nt(qj.astype(BF16), kj.astype(BF16))
        att = jnp.where(row >= e, att, 0.0)
        o = o + _dot(att.astype(BF16), v16[jb * sub:e, :])
    zero_tile = jnp.zeros((tile, LANES), F32)
    q_parts, k_parts = [], []
    for r0 in range(0, c_rows, sub):
        r_ref = b[r0 + tile - 1:r0 + tile, :]
        first, second = slice(r0, r0 + tile), slice(r0 + tile, r0 + sub)
        k_parts += [kk[first, :] * jnp.exp(r_ref - b[first, :]), zero_tile]
        q_parts += [zero_tile, q[second, :] * jnp.exp(b[second, :] - r_ref)]
    att = _dot_nt(jnp.concatenate(q_parts, axis=0).astype(BF16), jnp.concatenate(k_parts, axis=0).astype(BF16))
    att = jnp.where(same_sub, att, 0.0)
    decay = None
    for dlt in range(tile):
        if dlt == 0:
            w = q * kk
        else:
            gate = _roll_in_tiles(f, dlt - 1)
            decay = gate if decay is None else decay * gate
            w = q * _roll_in_tiles(kk, dlt) * decay
        att = att + jnp.where(diag[dlt], jnp.sum(w, axis=-1, keepdims=True), 0.0)
    o = o + _dot(att.astype(BF16), v16)
    return _rms(o) * gn * _silu(g), st_new


def _hgrn_prompt_kernel(q_ref, f_ref, v_ref, g_ref, lbl_ref, gn_ref, y_ref, sfin_ref, st_scr,
                        *, rows, group):
    t = pl.program_id(2)
    c_rows = HGRN_CHUNK

    @pl.when(t == 0)
    def _():
        st_scr[...] = jnp.zeros_like(st_scr)

    lb = _lower_bound([lbl_ref[l:l + 1, :] for l in range(lbl_ref.shape[0])])
    masks = _hgrn_masks()

    def chunk(c, carry):
        r0 = pl.multiple_of(c * c_rows, c_rows)
        for gi in range(group):
            sl = slice(gi * LANES, (gi + 1) * LANES)
            y, st_new = _hgrn_chunk(q_ref[pl.ds(r0, c_rows), sl], f_ref[pl.ds(r0, c_rows), sl],
                                    v_ref[pl.ds(r0, c_rows), sl], g_ref[pl.ds(r0, c_rows), sl],
                                    lb[:, sl], gn_ref[...], st_scr[gi], masks)
            st_scr[gi] = st_new
            y_ref[pl.ds(r0, c_rows), sl] = y
        return carry

    lax.fori_loop(0, rows // c_rows, chunk, 0)

    @pl.when(t == pl.num_programs(2) - 1)
    def _():
        for gi in range(group):
            sfin_ref[gi] = st_scr[gi].T


def _hgrn_prompt(zq, lb_logits, hg_norm, n_batch, seq, n_heads, rows=512, group=8):
    m = zq.shape[1]
    nt = seq // rows
    group = math.gcd(group, n_heads)
    gw = group * LANES

    def zspec(part):
        return pl.BlockSpec((None, rows, gw), lambda b, h, t: (part, b * nt + t, h))

    return pl.pallas_call(
        functools.partial(_hgrn_prompt_kernel, rows=rows, group=group),
        grid=(n_batch, n_heads // group, nt),
        in_specs=[zspec(0), zspec(1), zspec(2), zspec(3),
                  pl.BlockSpec((lb_logits.shape[0], gw), lambda b, h, t: (0, h)),
                  pl.BlockSpec((1, LANES), lambda b, h, t: (0, 0))],
        out_specs=[pl.BlockSpec((rows, gw), lambda b, h, t: (b * nt + t, h)),
                   pl.BlockSpec((None, group, LANES, LANES), lambda b, h, t: (b, h, 0, 0))],
        out_shape=[jax.ShapeDtypeStruct((m, n_heads * LANES), F32),
                   jax.ShapeDtypeStruct((n_batch, n_heads, LANES, LANES), F32)],
        scratch_shapes=[pltpu.VMEM((group, LANES, LANES), F32)],
        compiler_params=_params("arbitrary", "arbitrary", "arbitrary"),
        name="hgrn_prompt",
    )(zq, zq, zq, zq, lb_logits, hg_norm)


def _hgrn_sample_kernel(q_ref, f_ref, v_ref, g_ref, lbl_ref, gn_ref, s_ref, y_ref, so_ref,
                        *, n_heads):
    lb = _lower_bound([lbl_ref[l] for l in range(lbl_ref.shape[0])])
    q = q_ref[...]
    v = v_ref[...]
    f = lb + (1.0 - lb) * _sigmoid(f_ref[...])
    kk = 1.0 - f
    pad = jnp.zeros((LANES - 3 * n_heads, LANES), F32)
    cols = jnp.concatenate([f, kk, q, pad], axis=0).T
    o_rows = []
    for h in range(n_heads):
        f_col = cols[:, h:h + 1]
        k_col = cols[:, n_heads + h:n_heads + h + 1]
        q_col = cols[:, 2 * n_heads + h:2 * n_heads + h + 1]
        s_new = f_col * s_ref[h] + k_col * v[h:h + 1, :]
        so_ref[h] = s_new
        o_rows.append(jnp.sum(q_col * s_new, axis=0, keepdims=True))
    o = jnp.concatenate(o_rows, axis=0)
    y_ref[...] = _rms(o) * gn_ref[...] * _silu(g_ref[...])


def _hgrn_sample(zq4, lb_logits3, hg_norm, state):
    n_seq, n_heads = state.shape[0], state.shape[1]

    def zspec(part):
        return pl.BlockSpec((None, None, n_heads, LANES), lambda b: (part, b, 0, 0))

    return pl.pallas_call(
        functools.partial(_hgrn_sample_kernel, n_heads=n_heads),
        grid=(n_seq,),
        in_specs=[zspec(0), zspec(1), zspec(2), zspec(3),
                  pl.BlockSpec(lb_logits3.shape, lambda b: (0, 0, 0)),
                  pl.BlockSpec((1, LANES), lambda b: (0, 0)),
                  pl.BlockSpec((None, n_heads, LANES, LANES), lambda b: (b, 0, 0, 0))],
        out_specs=[pl.BlockSpec((None, n_heads, LANES), lambda b: (b, 0, 0)),
                   pl.BlockSpec((None, n_heads, LANES, LANES), lambda b: (b, 0, 0, 0))],
        out_shape=[jax.ShapeDtypeStruct((n_seq, n_heads, LANES), F32),
                   jax.ShapeDtypeStruct(state.shape, F32)],
        compiler_params=_params("arbitrary"),
        name="hgrn_sample",
    )(zq4, zq4, zq4, zq4, lb_logits3, hg_norm, state)


def _attn_prompt_kernel(lq1_ref, lk1_ref, lq2_ref, lk2_ref, an_ref, q_ref, k_ref, v_ref, y_ref,
                        k16_scr, vt_scr, qm_scr, m_scr, l_scr, acc_scr, *, lam_init, tq, tk, qk_dim):
    qi = pl.program_id(2)
    n_kv = vt_scr.shape[0]
    per_q = tq // tk

    @pl.when(qi == 0)
    def _():
        k16_scr[...] = k_ref[...].astype(BF16)
        for t in range(n_kv):
            vt_scr[t] = v_ref[t * tk:(t + 1) * tk, :].T.astype(BF16)

    q = q_ref[...] * (qk_dim ** -0.5 * LOG2_E)
    lane = lax.broadcasted_iota(jnp.int32, (1, LANES), 1)
    qm_scr[:tq, :] = jnp.where(lane < qk_dim, q, 0.0).astype(BF16)
    qm_scr[tq:, :] = jnp.where(lane >= qk_dim, q, 0.0).astype(BF16)
    m_scr[...] = jnp.full_like(m_scr, -jnp.inf)
    l_scr[...] = jnp.zeros_like(l_scr)
    acc_scr[...] = jnp.zeros_like(acc_scr)

    def kv_tile(kj, diagonal):
        r0 = pl.multiple_of(kj * tk, tk)
        st = _dot_nt(k16_scr[pl.ds(r0, tk), :], qm_scr[...])
        if diagonal is not None:
            key = lax.broadcasted_iota(jnp.int32, (tk, 2 * tq), 0) + diagonal * tk
            qry = lax.broadcasted_iota(jnp.int32, (tk, 2 * tq), 1) % tq
            st = jnp.where(key <= qry, st, -jnp.inf)
        m_old = m_scr[...]
        m_new = jnp.maximum(m_old, jnp.max(st, axis=0, keepdims=True))
        p = jnp.exp2(st - m_new)
        corr = jnp.exp2(m_old - m_new)
        l_scr[...] = l_scr[...] * corr + jnp.sum(p.reshape(tk // 8, 8, 2 * tq), axis=0)
        acc_scr[...] = acc_scr[...] * corr + _dot(vt_scr[kj], p.astype(BF16))
        m_scr[...] = m_new

    def body(kj, carry):
        kv_tile(kj, None)
        return carry

    lax.fori_loop(0, qi * per_q, body, 0)
    for dg in range(per_q):
        kv_tile(qi * per_q + dg, dg)

    lam = _lam_value(lq1_ref[...], lk1_ref[...], lq2_ref[...], lk2_ref[...], lam_init)
    o = acc_scr[...] / jnp.sum(l_scr[...], axis=0, keepdims=True)
    o = (o[:, :tq] - lam * o[:, tq:]).T
    y_ref[...] = _rms(o) * an_ref[...] * (1.0 - lam_init)


def _attn_prompt(lam_params, at_norm, zq, k, v, n_batch, seq, n_heads, lam_init, tq=512, tk=512):
    m = k.shape[0]
    nt = seq // tq
    qk_dim = lam_params[0].shape[1]
    small = [pl.BlockSpec(p.shape, lambda b, h, i: (0, 0)) for p in lam_params]
    kv_spec = pl.BlockSpec((seq, LANES), lambda b, h, i: (b, h))

    return pl.pallas_call(
        functools.partial(_attn_prompt_kernel, lam_init=lam_init, tq=tq, tk=tk, qk_dim=qk_dim),
        grid=(n_batch, n_heads, nt),
        in_specs=small + [pl.BlockSpec((1, LANES), lambda b, h, i: (0, 0)),
                          pl.BlockSpec((None, tq, LANES), lambda b, h, i: (4, b * nt + i, h)),
                          kv_spec, kv_spec],
        out_specs=pl.BlockSpec((tq, LANES), lambda b, h, i: (b * nt + i, h)),
        out_shape=jax.ShapeDtypeStruct((m, n_heads * LANES), F32),
        scratch_shapes=[pltpu.VMEM((seq, LANES), BF16), pltpu.VMEM((seq // tk, LANES, tk), BF16),
                        pltpu.VMEM((2 * tq, LANES), BF16), pltpu.VMEM((1, 2 * tq), F32),
                        pltpu.VMEM((8, 2 * tq), F32), pltpu.VMEM((LANES, 2 * tq), F32)],
        compiler_params=_params("arbitrary", "arbitrary", "arbitrary"),
        name="attn_prompt",
    )(*lam_params, at_norm, zq, k, v)


def _attn_sample_kernel(pt_ref, lq1_ref, lk1_ref, lq2_ref, lk2_ref, an_ref, q_ref, kn_ref, vn_ref,
                        *rest, lam_init, n_heads, qk_dim, pages):
    k_refs = rest[:pages]
    v_refs = rest[pages:2 * pages]
    y_ref, m_scr, l_scr, acc_scr = rest[2 * pages:]
    j = pl.program_id(1)
    page_rows = k_refs[0].shape[0] * n_heads

    q = q_ref[...] * (qk_dim ** -0.5)
    lane = lax.broadcasted_iota(jnp.int32, (1, LANES), 1)
    qm = jnp.concatenate([jnp.where(lane < qk_dim, q, 0.0), jnp.where(lane >= qk_dim, q, 0.0)], axis=0)

    @pl.when(j == 0)
    def _():
        kn = kn_ref[...]
        vn = vn_ref[...]
        m_scr[...] = jnp.sum(qm * jnp.concatenate([kn, kn], axis=0), axis=-1, keepdims=True)
        l_scr[...] = jnp.ones_like(l_scr)
        acc_scr[...] = jnp.concatenate([vn, vn], axis=0)

    qm16 = qm.astype(BF16)
    own_head = (lax.broadcasted_iota(jnp.int32, (2 * n_heads, page_rows), 1) % n_heads
                == lax.broadcasted_iota(jnp.int32, (2 * n_heads, page_rows), 0) % n_heads)
    scores = [jnp.where(own_head, _dot_nt(qm16, k_refs[p][...].reshape(page_rows, LANES).astype(BF16)),
                        -jnp.inf) for p in range(pages)]
    m_old = m_scr[...]
    m_new = functools.reduce(jnp.maximum, [jnp.max(s, axis=-1, keepdims=True) for s in scores] + [m_old])
    corr = jnp.exp(m_old - m_new)
    l_new = l_scr[...] * corr
    acc = acc_scr[...] * corr
    for p in range(pages):
        pr = jnp.exp(scores[p] - m_new)
        l_new = l_new + jnp.sum(pr, axis=-1, keepdims=True)
        acc = acc + _dot(pr.astype(BF16), v_refs[p][...].reshape(page_rows, LANES).astype(BF16))
    l_scr[...] = l_new
    acc_scr[...] = acc
    m_scr[...] = m_new

    @pl.when(j == pl.num_programs(1) - 1)
    def _():
        lam = _lam_value(lq1_ref[...], lk1_ref[...], lq2_ref[...], lk2_ref[...], lam_init)
        o = acc_scr[...] / l_scr[...]
        w = o[:n_heads] - lam * o[n_heads:]
        y_ref[...] = _rms(w) * an_ref[...] * (1.0 - lam_init)


def _attn_sample(page_table, lam_params, at_norm, q3, kn3, vn3, cache_k, cache_v, lam_init, pages=8):
    n_seq, n_pages = page_table.shape
    _, _, page_size, n_heads, _ = cache_k.shape
    qk_dim = lam_params[0].shape[1]
    assert n_pages % pages == 0
    small = [pl.BlockSpec(p.shape, lambda b, j, pt: (0, 0)) for p in lam_params]
    row_spec = pl.BlockSpec((None, n_heads, LANES), lambda b, j, pt: (b, 0, 0))

    def page_spec(p):
        return pl.BlockSpec(
            (None, None, page_size, n_heads, LANES),
            lambda b, j, pt: (0, pt[b * n_pages + j * pages + p], 0, 0, 0))

    grid_spec = pltpu.PrefetchScalarGridSpec(
        num_scalar_prefetch=1,
        grid=(n_seq, n_pages // pages),
        in_specs=(small + [pl.BlockSpec((1, LANES), lambda b, j, pt: (0, 0)), row_spec, row_spec, row_spec]
                  + [page_spec(p) for p in range(pages)] + [page_spec(p) for p in range(pages)]),
        out_specs=row_spec,
        scratch_shapes=[pltpu.VMEM((2 * n_heads, 1), F32), pltpu.VMEM((2 * n_heads, 1), F32),
                        pltpu.VMEM((2 * n_heads, LANES), F32)],
    )
    return pl.pallas_call(
        functools.partial(_attn_sample_kernel, lam_init=lam_init, n_heads=n_heads, qk_dim=qk_dim,
                          pages=pages),
        grid_spec=grid_spec,
        out_shape=jax.ShapeDtypeStruct((n_seq, n_heads, LANES), F32),
        compiler_params=_params("arbitrary", "arbitrary"),
        name="attn_sample",
    )(page_table.reshape(-1), *lam_params, at_norm, q3, kn3, vn3,
      *([cache_k] * pages), *([cache_v] * pages))


def _outproj_kernel(x_ref, yh_ref, ya_ref, g1_ref, sh_ref, sc_ref, n2_ref, w_ref, o_ref, h_ref):
    hw = yh_ref.shape[1]
    rows = min(256, x_ref.shape[0])
    for r in range(x_ref.shape[0] // rows):
        rs = slice(r * rows, (r + 1) * rows)
        acc = (_dot(yh_ref[rs, :].astype(BF16), w_ref[:hw, :])
               + _dot(ya_ref[rs, :].astype(BF16), w_ref[hw:, :]))
        x1 = x_ref[rs, :] + g1_ref[...] * acc
        o_ref[rs, :] = x1
        h_ref[rs, :] = (_rms(x1) * n2_ref[...] * (1.0 + sc_ref[...]) + sh_ref[...]).astype(BF16)


def _outproj(x2, y_hg, y_at, mod, mod_spec, norm2, w_out, tm):
    m, d = x2.shape
    return pl.pallas_call(
        _outproj_kernel,
        grid=(m // tm,),
        in_specs=[pl.BlockSpec((tm, d), lambda i: (i, 0)),
                  pl.BlockSpec((tm, y_hg.shape[1]), lambda i: (i, 0)),
                  pl.BlockSpec((tm, y_at.shape[1]), lambda i: (i, 0)),
                  mod_spec(2), mod_spec(3), mod_spec(4),
                  pl.BlockSpec((1, d), lambda i: (0, 0)),
                  pl.BlockSpec(w_out.shape, lambda i: (0, 0))],
        out_specs=[pl.BlockSpec((tm, d), lambda i: (i, 0)),
                   pl.BlockSpec((tm, d), lambda i: (i, 0))],
        out_shape=[jax.ShapeDtypeStruct((m, d), F32),
                   jax.ShapeDtypeStruct((m, d), BF16)],
        compiler_params=_params("arbitrary"),
        name="outproj",
    )(x2, y_hg, y_at, mod, mod, mod, norm2, w_out)


def _conv_gate(conv, u):
    return (_silu(conv) * u).astype(BF16)


def _ffn_up_kernel(h_ref, halo_ref, wa_ref, wu_ref, cw_ref, cb_ref, g_ref, tail_ref, h_scr, a_scr,
                   *, tm, tiles_per_seq, sub_blocks):
    i = pl.program_id(0)

    @pl.when(pl.program_id(1) == 0)
    def _():
        h_scr[:HALO, :] = halo_ref[...]
        h_scr[HALO:, :] = h_ref[...]

    rows = tm // sub_blocks
    for r in range(sub_blocks):
        lo = HALO + r * rows
        if r == 0:
            a_scr[:lo + rows, :] = _dot(h_scr[:lo + rows, :], wa_ref[...])

            @pl.when(i % tiles_per_seq == 0)
            def _():
                a_scr[:HALO, :] = jnp.zeros((HALO, a_scr.shape[1]), F32)
        else:
            a_scr[lo:lo + rows, :] = _dot(h_scr[lo:lo + rows, :], wa_ref[...])
        u = _dot(h_scr[lo:lo + rows, :], wu_ref[...])
        conv = (cb_ref[...] + a_scr[lo - 2:lo - 2 + rows, :] * cw_ref[0:1, :]
                + a_scr[lo - 1:lo - 1 + rows, :] * cw_ref[1:2, :]
                + a_scr[lo:lo + rows, :] * cw_ref[2:3, :])
        g_ref[r * rows:(r + 1) * rows, :] = _conv_gate(conv, u)
    tail_ref[...] = a_scr[HALO + tm - 2:, :]


def _ffn_up(h2, w_up, conv_w, conv_b, seq, tm=1024, tf=512, sub_blocks=4):
    m, d = h2.shape
    ff = w_up.shape[1] // 2
    nf = ff // tf
    tm = min(tm, seq)
    tps = seq // tm
    assert conv_w.shape[0] == 3 and ff % tf == 0 and seq % tm == 0
    g, tails = pl.pallas_call(
        functools.partial(_ffn_up_kernel, tm=tm, tiles_per_seq=tps, sub_blocks=sub_blocks),
        grid=(m // tm, nf),
        in_specs=[pl.BlockSpec((tm, d), lambda i, f: (i, 0)),
                  pl.BlockSpec((HALO, d), lambda i, f: (jnp.maximum(i * (tm // HALO) - 1, 0), 0)),
                  pl.BlockSpec((d, tf), lambda i, f: (0, f)),
                  pl.BlockSpec((d, tf), lambda i, f: (0, nf + f)),
                  pl.BlockSpec((3, tf), lambda i, f: (0, f)),
                  pl.BlockSpec((1, tf), lambda i, f: (0, f))],
        out_specs=[pl.BlockSpec((tm, tf), lambda i, f: (i, f)),
                   pl.BlockSpec((None, 2, tf), lambda i, f: (i, 0, f))],
        out_shape=[jax.ShapeDtypeStruct((m, ff), BF16),
                   jax.ShapeDtypeStruct((m // tm, 2, ff), F32)],
        scratch_shapes=[pltpu.VMEM((HALO + tm, d), BF16), pltpu.VMEM((HALO + tm, tf), F32)],
        compiler_params=_params("arbitrary", "arbitrary"),
        name="ffn_up",
    )(h2, h2, w_up, w_up, conv_w, conv_b)
    return g, tails[tps - 1::tps]


def _ffn_down_kernel(g_ref, x_ref, g2_ref, wd_ref, fn_ref, y_ref, ss_scr, *, tn, sub_blocks):
    n = pl.program_id(1)
    d = y_ref.shape[1]

    @pl.when(n == 0)
    def _():
        ss_scr[...] = jnp.zeros_like(ss_scr)

    rows = y_ref.shape[0] // sub_blocks
    for r in range(sub_blocks):
        rs = slice(r * rows, (r + 1) * rows)
        x2 = x_ref[rs, :] + g2_ref[...] * _dot(g_ref[rs, :], wd_ref[...])
        ss_scr[rs, :] += jnp.sum(x2 * x2, axis=-1, keepdims=True)
        y_ref[rs, pl.ds(pl.multiple_of(n * tn, tn), tn)] = x2

    @pl.when(n == d // tn - 1)
    def _():
        y_ref[...] = y_ref[...] * lax.rsqrt(ss_scr[...] / d + EPS) * fn_ref[...]


def _ffn_down(g, x1, mod_p, w_down, final_norm, seq, tm=1024, tn=256, sub_blocks=4):
    m, d = x1.shape
    ff = w_down.shape[0]
    tm = min(tm, seq)
    tps = seq // tm
    nd = d // tn
    assert d % tn == 0 and seq % tm == 0
    return pl.pallas_call(
        functools.partial(_ffn_down_kernel, tn=tn, sub_blocks=sub_blocks),
        grid=(m // tm, nd),
        in_specs=[pl.BlockSpec((tm, ff), lambda i, n: (i, 0)),
                  pl.BlockSpec((tm, tn), lambda i, n: (i, n)),
                  pl.BlockSpec((None, 1, tn), lambda i, n: (i // tps, 0, 5 * nd + n)),
                  pl.BlockSpec((ff, tn), lambda i, n: (0, n)),
                  pl.BlockSpec((1, d), lambda i, n: (0, 0))],
        out_specs=pl.BlockSpec((tm, d), lambda i, n: (i, 0)),
        out_shape=jax.ShapeDtypeStruct((m, d), F32),
        scratch_shapes=[pltpu.VMEM((tm, 1), F32)],
        compiler_params=_params("arbitrary", "arbitrary"),
        name="ffn_down",
    )(g, x1, mod_p, w_down, final_norm)


def _ffn_sample_kernel(x_ref, h_ref, g2_ref, p0_ref, p1_ref, wa_ref, wu_ref, cw_ref,
                       cb_ref, wd_ref, fn_ref, y_ref, a_ref, acc_scr):
    f = pl.program_id(0)

    @pl.when(f == 0)
    def _():
        acc_scr[...] = jnp.zeros_like(acc_scr)

    a = _dot(h_ref[...], wa_ref[...])
    u = _dot(h_ref[...], wu_ref[...])
    conv = (cb_ref[...] + p0_ref[...] * cw_ref[0:1, :] + p1_ref[...] * cw_ref[1:2, :]
            + a * cw_ref[2:3, :])
    acc_scr[...] += _dot(_conv_gate(conv, u), wd_ref[...])
    a_ref[...] = a

    @pl.when(f == pl.num_programs(0) - 1)
    def _():
        y_ref[...] = _rms(x_ref[...] + g2_ref[...] * acc_scr[...]) * fn_ref[...]


def _ffn_sample(x1, h2, mod_s, prev2, w_up, conv_w, conv_b, w_down, final_norm, tf=512):
    m, d = x1.shape
    ff = w_down.shape[0]
    nf = ff // tf

    return pl.pallas_call(
        _ffn_sample_kernel,
        grid=(nf,),
        in_specs=[pl.BlockSpec((m, d), lambda f: (0, 0)),
                  pl.BlockSpec((m, d), lambda f: (0, 0)),
                  pl.BlockSpec((m, d), lambda f: (0, 5)),
                  pl.BlockSpec((m, tf), lambda f: (0, f)),
                  pl.BlockSpec((m, tf), lambda f: (0, nf + f)),
                  pl.BlockSpec((d, tf), lambda f: (0, f)),
                  pl.BlockSpec((d, tf), lambda f: (0, nf + f)),
                  pl.BlockSpec((3, tf), lambda f: (0, f)),
                  pl.BlockSpec((1, tf), lambda f: (0, f)),
                  pl.BlockSpec((tf, d), lambda f: (f, 0)),
                  pl.BlockSpec((1, d), lambda f: (0, 0))],
        out_specs=[pl.BlockSpec((m, d), lambda f: (0, 0)),
                   pl.BlockSpec((m, tf), lambda f: (0, f))],
        out_shape=[jax.ShapeDtypeStruct((m, d), F32),
                   jax.ShapeDtypeStruct((m, ff), F32)],
        scratch_shapes=[pltpu.VMEM((m, d), F32)],
        compiler_params=_params("arbitrary"),
        name="ffn_sample",
    )(x1, h2, mod_s, prev2, prev2, w_up, w_up, conv_w, conv_b, w_down, final_norm)


def _rope_tables(pos, qk_dim):
    rot = qk_dim // 4
    half = rot // 2
    inv_freq = ROPE_THETA ** (-jnp.arange(half, dtype=F32) / half)
    ang = pos.astype(F32)[:, None] * inv_freq[None, :]
    cos, sin = jnp.cos(ang), jnp.sin(ang)
    ones = jnp.ones((pos.shape[0], qk_dim - rot), F32)
    zeros = jnp.zeros((pos.shape[0], qk_dim - rot), F32)
    zh = jnp.zeros_like(sin)
    cos_t = jnp.concatenate([cos, cos, ones] * 2, axis=1)
    sin_a = jnp.concatenate([zh, sin, zeros] * 2, axis=1)
    sin_b = jnp.concatenate([-sin, zh, zeros] * 2, axis=1)
    return cos_t, sin_a, sin_b


def kernel(x_prompt, x_sample, c_prompt, c_sample, cache_k, cache_v, state_hgrn, state_conv, page_table,
           w_ada, b_ada, norm1, norm2, w_in, hg_lb_logits, hg_norm, lam_q1, lam_k1, lam_q2, lam_k2,
           at_norm, w_out, w_up, conv_w, conv_b, w_down, final_norm):
    n_batch, seq, d = x_prompt.shape
    n_seq = x_sample.shape[0]
    depth = w_in.shape[0]
    assert depth == 1 and x_sample.shape[1] == 1
    n_heads = state_hgrn.shape[2]
    assert cache_k.shape[3] == n_heads and cache_k.shape[4] == LANES and state_hgrn.shape[3] == LANES
    qk_dim = lam_q1.shape[1]
    past_len = page_table.shape[1] * cache_k.shape[2]
    lam_init = 0.8 - 0.6 * math.exp(-0.3 * 0)
    ff = w_down.shape[1]
    width = n_heads * LANES
    lam_params = [lam_q1, lam_k1, lam_q2, lam_k2]

    w_in16 = w_in[0].astype(BF16)
    w_out16 = w_out[0].astype(BF16)
    w_up16 = w_up[0].astype(BF16)
    w_down16 = w_down[0].astype(BF16)

    mod = _adaln(jnp.concatenate([c_prompt, c_sample], axis=0), w_ada[0], b_ada[0])
    mod_p = mod[:n_batch].reshape(n_batch, 1, N_MOD * d)
    mod_s = mod[n_batch:]

    tm_in = min(1024, seq)
    tps_in = seq // tm_in
    tm_out = min(512, seq)
    tps_out = seq // tm_out
    xp = x_prompt.reshape(n_batch * seq, d)
    tables_p = _rope_tables(jnp.arange(seq, dtype=jnp.int32), qk_dim)

    def pspec(part):
        return pl.BlockSpec((None, 1, d), lambda i, j: (i // tps_in, 0, part))

    zq, k_p, v_p = _inproj(xp, mod_p, mod_p, (pspec(0), pspec(1)), norm1, w_in16, tables_p,
                           pl.BlockSpec((tm_in, LANES), lambda i, j: (i % tps_in, 0)), tm_in, n_heads)
    y_hg, hgrn_p = _hgrn_prompt(zq, hg_lb_logits, hg_norm, n_batch, seq, n_heads)
    y_at = _attn_prompt(lam_params, at_norm, zq, k_p, v_p, n_batch, seq, n_heads, lam_init)
    x1, h2 = _outproj(xp, y_hg, y_at, mod_p,
                      lambda part: pl.BlockSpec((None, 1, d), lambda i: (i // tps_out, 0, part)),
                      norm2, w_out16, tm_out)
    g_p, conv_p = _ffn_up(h2, w_up16, conv_w[0], conv_b, seq)
    y_p = _ffn_down(g_p, x1, mod_p, w_down16, final_norm.reshape(1, d), seq)

    xs = x_sample.reshape(n_seq, d)
    tables_s = _rope_tables(jnp.full((n_seq,), past_len, jnp.int32), qk_dim)

    def sspec(part):
        return pl.BlockSpec((n_seq, d), lambda i, j: (0, part))

    zq_s, k_s, v_s = _inproj(xs, mod_s, mod_s, (sspec(0), sspec(1)), norm1, w_in16, tables_s,
                             pl.BlockSpec((n_seq, LANES), lambda i, j: (0, 0)), n_seq, n_heads)
    zq_s4 = zq_s.reshape(5, n_seq, n_heads, LANES)
    y_hg_s, hgrn_s = _hgrn_sample(zq_s4, hg_lb_logits.reshape(-1, n_heads, LANES), hg_norm, state_hgrn[0])
    y_at_s = _attn_sample(page_table, lam_params, at_norm, zq_s4[4], k_s.reshape(n_seq, n_heads, LANES),
                          v_s.reshape(n_seq, n_heads, LANES), cache_k, cache_v, lam_init)
    x1_s, h2_s = _outproj(xs, y_hg_s.reshape(n_seq, width), y_at_s.reshape(n_seq, width), mod_s,
                          lambda part: pl.BlockSpec((n_seq, d), lambda i: (0, part)),
                          norm2, w_out16, n_seq)
    prev = state_conv[0]
    y_s, a_s = _ffn_sample(x1_s, h2_s, mod_s, prev.reshape(n_seq, 2 * ff), w_up16, conv_w[0], conv_b,
                           w_down16, final_norm.reshape(1, d))
    conv_s = jnp.stack([prev[:, 1, :], a_s], axis=1)

    return (y_p.reshape(n_batch, seq, d), y_s.reshape(n_seq, 1, d),
            k_p.reshape(1, n_batch, seq, n_heads, LANES), v_p.reshape(1, n_batch, seq, n_heads, LANES),
            k_s.reshape(1, n_seq, 1, n_heads, LANES), v_s.reshape(1, n_seq, 1, n_heads, LANES),
            hgrn_p[None], hgrn_s[None], conv_p[None], conv_s[None])
```

```python
import functools
import math

import jax
import jax.numpy as jnp
from jax import lax
from jax.experimental import pallas as pl
from jax.experimental.pallas import tpu as pltpu

EPS = 1e-6
ROPE_THETA = 500000.0
LOG2_E = math.log2(math.e)
N_MOD = 6
LANES = 128
HGRN_CHUNK = 64
HGRN_SUB = 16
HGRN_TILE = 8
HALO = 16
VMEM_LIMIT_BYTES = 56 * 1024 * 1024

F32 = jnp.float32
BF16 = jnp.bfloat16


def _params(*sem):
    return pltpu.CompilerParams(dimension_semantics=sem, vmem_limit_bytes=VMEM_LIMIT_BYTES)


def _silu(x):
    return x / (1.0 + jnp.exp(-x))


def _sigmoid(x):
    return 1.0 / (1.0 + jnp.exp(-x))


def _rms(x):
    return x * lax.rsqrt(jnp.mean(x * x, axis=-1, keepdims=True) + EPS)


def _dot(a, b):
    return jnp.dot(a, b, preferred_element_type=F32)


def _dot_nt(a, b):
    return lax.dot_general(a, b, (((1,), (1,)), ((), ())), preferred_element_type=F32)


def _lam_value(lq1, lk1, lq2, lk2, lam_init):
    return (jnp.exp(jnp.sum(lq1 * lk1, axis=-1, keepdims=True))
            - jnp.exp(jnp.sum(lq2 * lk2, axis=-1, keepdims=True)) + lam_init)


def _lower_bound(layer_logits):
    mx = functools.reduce(jnp.maximum, layer_logits)
    es = [jnp.exp(r - mx) for r in layer_logits]
    return es[0] / functools.reduce(jnp.add, es)


def _adaln_kernel(c_ref, w_ref, b_ref, o_ref):
    s = _silu(c_ref[...]).astype(BF16)
    o_ref[...] = _dot(s, w_ref[...].astype(BF16)) + b_ref[...]


def _adaln(c_all, w_ada, b_ada, tn=512):
    rows, d = c_all.shape
    n = w_ada.shape[1]
    assert n % tn == 0
    return pl.pallas_call(
        _adaln_kernel,
        grid=(n // tn,),
        in_specs=[pl.BlockSpec((rows, d), lambda j: (0, 0)),
                  pl.BlockSpec((d, tn), lambda j: (0, j)),
                  pl.BlockSpec((1, tn), lambda j: (0, j))],
        out_specs=pl.BlockSpec((rows, tn), lambda j: (0, j)),
        out_shape=jax.ShapeDtypeStruct((rows, n), F32),
        compiler_params=_params("arbitrary"),
        name="adaln",
    )(c_all, w_ada, b_ada.reshape(1, n))


def _rope(blk, cos_t, sin_a, sin_b):
    return (blk * cos_t + pltpu.roll(blk, 8, axis=1) * sin_a
            + pltpu.roll(blk, LANES - 8, axis=1) * sin_b)


def _inproj_kernel(x_ref, sh_ref, sc_ref, g_ref, w_ref, cos_ref, sa_ref, sb_ref,
                   zq_ref, k_ref, v_ref, h_scr, *, split, sub_blocks):
    j = pl.program_id(1)
    part = j // split

    tm, tn = zq_ref.shape
    rows = tm // sub_blocks

    def rows_of(ref, rs):
        return ref[...] if ref.shape[0] == 1 else ref[rs, :]

    def project(dst, rotary=False, modulate=False):
        for r in range(sub_blocks):
            rs = slice(r * rows, (r + 1) * rows)
            if modulate:
                h = _rms(x_ref[rs, :]) * g_ref[...] * (1.0 + rows_of(sc_ref, rs)) + rows_of(sh_ref, rs)
                h_scr[rs, :] = h.astype(BF16)
            acc = _dot(h_scr[rs, :], w_ref[...])
            if rotary:
                for hh in range(tn // LANES):
                    sl = slice(hh * LANES, (hh + 1) * LANES)
                    dst[rs, sl] = _rope(acc[:, sl], cos_ref[rs, :], sa_ref[rs, :], sb_ref[rs, :])
            else:
                dst[rs, :] = acc

    @pl.when(j == 0)
    def _():
        project(zq_ref, modulate=True)

    @pl.when((j > 0) & (part < 4))
    def _():
        project(zq_ref)

    @pl.when(part == 4)
    def _():
        project(zq_ref, rotary=True)

    @pl.when(part == 5)
    def _():
        project(k_ref, rotary=True)

    @pl.when(part == 6)
    def _():
        project(v_ref)


def _inproj(x2, sh, sc, mod_spec, norm_g, w_in, tables, tab_spec, tm, n_heads, split=2):
    m, d = x2.shape
    width = n_heads * LANES
    assert w_in.shape[1] == 7 * width and n_heads % split == 0
    tn = width // split
    sh_spec, sc_spec = mod_spec

    def zq_map(i, j):
        jz = jnp.minimum(j, 5 * split - 1)
        return (jz // split, i, jz % split)

    return pl.pallas_call(
        functools.partial(_inproj_kernel, split=split, sub_blocks=max(1, tm // 256)),
        grid=(m // tm, 7 * split),
        in_specs=[pl.BlockSpec((tm, d), lambda i, j: (i, 0)),
                  sh_spec, sc_spec,
                  pl.BlockSpec((1, d), lambda i, j: (0, 0)),
                  pl.BlockSpec((d, tn), lambda i, j: (0, j)),
                  tab_spec, tab_spec, tab_spec],
        out_specs=[pl.BlockSpec((None, tm, tn), zq_map),
                   pl.BlockSpec((tm, tn), lambda i, j: (i, jnp.clip(j - 5 * split, 0, split - 1))),
                   pl.BlockSpec((tm, tn), lambda i, j: (i, jnp.clip(j - 6 * split, 0, split - 1)))],
        out_shape=[jax.ShapeDtypeStruct((5, m, width), F32),
                   jax.ShapeDtypeStruct((m, width), F32),
                   jax.ShapeDtypeStruct((m, width), F32)],
        scratch_shapes=[pltpu.VMEM((tm, d), BF16)],
        compiler_params=_params("arbitrary", "arbitrary"),
        name="inproj",
    )(x2, sh, sc, norm_g, w_in, *tables)


def _roll_in_tiles(x, shift):
    if shift == 0:
        return x
    tiles = [pltpu.roll(x[r:r + HGRN_TILE, :], shift, axis=0) for r in range(0, x.shape[0], HGRN_TILE)]
    return jnp.concatenate(tiles, axis=0)


def _hgrn_masks():
    c_rows = HGRN_CHUNK
    ri = lax.broadcasted_iota(jnp.int32, (c_rows, c_rows), 0)
    ci = lax.broadcasted_iota(jnp.int32, (c_rows, c_rows), 1)
    tri = jnp.where(ri >= ci, 1.0, 0.0).astype(BF16)
    same_sub = ri // HGRN_SUB == ci // HGRN_SUB
    same_tile = ri // HGRN_TILE == ci // HGRN_TILE
    diag = [same_tile & (ci == ri - dlt) for dlt in range(HGRN_TILE)]
    row = lax.broadcasted_iota(jnp.int32, (c_rows, 1), 0)
    return tri, same_sub, diag, row


def _hgrn_chunk(q, hf, v, g, lb, gn, st, masks):
    c_rows, sub, tile = HGRN_CHUNK, HGRN_SUB, HGRN_TILE
    tri, same_sub, diag, row = masks
    f = lb + (1.0 - lb) * _sigmoid(hf)
    kk = 1.0 - f
    lf = jnp.log(f)
    hi = lf.astype(BF16)
    r1 = lf - hi.astype(F32)
    mid = r1.astype(BF16)
    lo = (r1 - mid.astype(F32)).astype(BF16)
    b = _dot(tri, hi) + _dot(tri, mid) + _dot(tri, lo)
    b_last = b[c_rows - 1:c_rows, :]

    o = _dot_nt((q * jnp.exp(b)).astype(BF16), st.astype(BF16))
    kd = kk * jnp.exp(b_last - b)
    st_new = st * jnp.exp(b_last) + _dot(v.T.astype(BF16), kd.astype(BF16))

    v16 = v.astype(BF16)
    for jb in range(c_rows // sub - 1):
        e = (jb + 1) * sub
        r_ref = b[e - 1:e, :]
        qj = q * jnp.exp(jnp.minimum(b - r_ref, 0.0))
        kj = kk[jb * sub:e, :] * jnp.exp(r_ref - b[jb * sub:e, :])
        att = _dot_nt(qj.astype(BF16), kj.astype(BF16))
        att = jnp.where(row >= e, att, 0.0)
        o = o + _dot(att.astype(BF16), v16[jb * sub:e, :])
    zero_tile = jnp.zeros((tile, LANES), F32)
    q_parts, k_parts = [], []
    for r0 in range(0, c_rows, sub):
        r_ref = b[r0 + tile - 1:r0 + tile, :]
        first, second = slice(r0, r0 + tile), slice(r0 + tile, r0 + sub)
        k_parts += [kk[first, :] * jnp.exp(r_ref - b[first, :]), zero_tile]
        q_parts += [zero_tile, q[second, :] * jnp.exp(b[second, :] - r_ref)]
    att = _dot_nt(jnp.concatenate(q_parts, axis=0).astype(BF16), jnp.concatenate(k_parts, axis=0).astype(BF16))
    att = jnp.where(same_sub, att, 0.0)
    decay = None
    for dlt in range(tile):
        if dlt == 0:
            w = q * kk
        else:
            gate = _roll_in_tiles(f, dlt - 1)
            decay = gate if decay is None else decay * gate
            w = q * _roll_in_tiles(kk, dlt) * decay
        att = att + jnp.where(diag[dlt], jnp.sum(w, axis=-1, keepdims=True), 0.0)
    o = o + _dot(att.astype(BF16), v16)
    return _rms(o) * gn * _silu(g), st_new


def _hgrn_prompt_kernel(q_ref, f_ref, v_ref, g_ref, lbl_ref, gn_ref, y_ref, sfin_ref, st_scr,
                        *, rows, group):
    t = pl.program_id(2)
    c_rows = HGRN_CHUNK

    @pl.when(t == 0)
    def _():
        st_scr[...] = jnp.zeros_like(st_scr)

    lb = _lower_bound([lbl_ref[l:l + 1, :] for l in range(lbl_ref.shape[0])])
    masks = _hgrn_masks()

    def chunk(c, carry):
        r0 = pl.multiple_of(c * c_rows, c_rows)
        for gi in range(group):
            sl = slice(gi * LANES, (gi + 1) * LANES)
            y, st_new = _hgrn_chunk(q_ref[pl.ds(r0, c_rows), sl], f_ref[pl.ds(r0, c_rows), sl],
                                    v_ref[pl.ds(r0, c_rows), sl], g_ref[pl.ds(r0, c_rows), sl],
                                    lb[:, sl], gn_ref[...], st_scr[gi], masks)
            st_scr[gi] = st_new
            y_ref[pl.ds(r0, c_rows), sl] = y
        return carry

    lax.fori_loop(0, rows // c_rows, chunk, 0)

    @pl.when(t == pl.num_programs(2) - 1)
    def _():
        for gi in range(group):
            sfin_ref[gi] = st_scr[gi].T


def _hgrn_prompt(zq, lb_logits, hg_norm, n_batch, seq, n_heads, rows=512, group=8):
    m = zq.shape[1]
    nt = seq // rows
    group = math.gcd(group, n_heads)
    gw = group * LANES

    def zspec(part):
        return pl.BlockSpec((None, rows, gw), lambda b, h, t: (part, b * nt + t, h))

    return pl.pallas_call(
        functools.partial(_hgrn_prompt_kernel, rows=rows, group=group),
        grid=(n_batch, n_heads // group, nt),
        in_specs=[zspec(0), zspec(1), zspec(2), zspec(3),
                  pl.BlockSpec((lb_logits.shape[0], gw), lambda b, h, t: (0, h)),
                  pl.BlockSpec((1, LANES), lambda b, h, t: (0, 0))],
        out_specs=[pl.BlockSpec((rows, gw), lambda b, h, t: (b * nt + t, h)),
                   pl.BlockSpec((None, group, LANES, LANES), lambda b, h, t: (b, h, 0, 0))],
        out_shape=[jax.ShapeDtypeStruct((m, n_heads * LANES), F32),
                   jax.ShapeDtypeStruct((n_batch, n_heads, LANES, LANES), F32)],
        scratch_shapes=[pltpu.VMEM((group, LANES, LANES), F32)],
        compiler_params=_params("arbitrary", "arbitrary", "arbitrary"),
        name="hgrn_prompt",
    )(zq, zq, zq, zq, lb_logits, hg_norm)


def _hgrn_sample_kernel(q_ref, f_ref, v_ref, g_ref, lbl_ref, gn_ref, s_ref, y_ref, so_ref,
                        *, n_heads):
    lb = _lower_bound([lbl_ref[l] for l in range(lbl_ref.shape[0])])
    pad = jnp.zeros((LANES - 3 * n_heads, LANES), F32)
    for s in range(q_ref.shape[0]):
        q = q_ref[s]
        v = v_ref[s]
        f = lb + (1.0 - lb) * _sigmoid(f_ref[s])
        kk = 1.0 - f
        cols = jnp.concatenate([f, kk, q, pad], axis=0).T
        o_rows = []
        for h in range(n_heads):
            f_col = cols[:, h:h + 1]
            k_col = cols[:, n_heads + h:n_heads + h + 1]
            q_col = cols[:, 2 * n_heads + h:2 * n_heads + h + 1]
            s_new = f_col * s_ref[s, h] + k_col * v[h:h + 1, :]
            so_ref[s, h] = s_new
            o_rows.append(jnp.sum(q_col * s_new, axis=0, keepdims=True))
        o = jnp.concatenate(o_rows, axis=0)
        y_ref[s] = _rms(o) * gn_ref[...] * _silu(g_ref[s])


def _hgrn_sample(zq4, lb_logits3, hg_norm, state, per_step=4):
    n_seq, n_heads = state.shape[0], state.shape[1]
    per_step = math.gcd(per_step, n_seq)

    def zspec(part):
        return pl.BlockSpec((None, per_step, n_heads, LANES), lambda b: (part, b, 0, 0))

    return pl.pallas_call(
        functools.partial(_hgrn_sample_kernel, n_heads=n_heads),
        grid=(n_seq // per_step,),
        in_specs=[zspec(0), zspec(1), zspec(2), zspec(3),
                  pl.BlockSpec(lb_logits3.shape, lambda b: (0, 0, 0)),
                  pl.BlockSpec((1, LANES), lambda b: (0, 0)),
                  pl.BlockSpec((per_step, n_heads, LANES, LANES), lambda b: (b, 0, 0, 0))],
        out_specs=[pl.BlockSpec((per_step, n_heads, LANES), lambda b: (b, 0, 0)),
                   pl.BlockSpec((per_step, n_heads, LANES, LANES), lambda b: (b, 0, 0, 0))],
        out_shape=[jax.ShapeDtypeStruct((n_seq, n_heads, LANES), F32),
                   jax.ShapeDtypeStruct(state.shape, F32)],
        compiler_params=_params("arbitrary"),
        name="hgrn_sample",
    )(zq4, zq4, zq4, zq4, lb_logits3, hg_norm, state)


def _attn_prompt_kernel(lq1_ref, lk1_ref, lq2_ref, lk2_ref, an_ref, q_ref, k_ref, v_ref, y_ref,
                        k16_scr, vt_scr, qm_scr, sa_scr, sb_scr, m_scr, l_scr, acc_scr, *, lam_init, tile,
                        qk_dim):
    qi = pl.program_id(2)
    n_kv = vt_scr.shape[0]
    tq = tk = tile

    @pl.when(qi == 0)
    def _():
        k16_scr[...] = k_ref[...].astype(BF16)
        for t in range(n_kv):
            vt_scr[t] = v_ref[t * tk:(t + 1) * tk, :].T.astype(BF16)

    q = q_ref[...] * (qk_dim ** -0.5 * LOG2_E)
    lane = lax.broadcasted_iota(jnp.int32, (1, LANES), 1)
    qm_scr[:tq, :] = jnp.where(lane < qk_dim, q, 0.0).astype(BF16)
    qm_scr[tq:, :] = jnp.where(lane >= qk_dim, q, 0.0).astype(BF16)
    m_scr[...] = jnp.full_like(m_scr, -jnp.inf)
    l_scr[...] = jnp.zeros_like(l_scr)
    acc_scr[...] = jnp.zeros_like(acc_scr)

    def scores(kj, dst):
        r0 = pl.multiple_of(kj * tk, tk)
        dst[...] = _dot_nt(k16_scr[pl.ds(r0, tk), :], qm_scr[...])

    def consume(kj, src, on_diagonal):
        st = src[...]
        if on_diagonal:
            key = lax.broadcasted_iota(jnp.int32, (tk, 2 * tq), 0)
            qry = lax.broadcasted_iota(jnp.int32, (tk, 2 * tq), 1) % tq
            st = jnp.where(key <= qry, st, -jnp.inf)
        m_old = m_scr[...]
        m_new = jnp.maximum(m_old, jnp.max(st, axis=0, keepdims=True))
        p = jnp.exp2(st - m_new)
        corr = jnp.exp2(m_old - m_new)
        l_scr[...] = l_scr[...] * corr + jnp.sum(p.reshape(tk // 8, 8, 2 * tq), axis=0)
        acc_scr[...] = acc_scr[...] * corr + _dot(vt_scr[kj], p.astype(BF16))
        m_scr[...] = m_new

    scores(0, sa_scr)

    def body(pair, carry):
        kj = 2 * pair
        scores(kj + 1, sb_scr)
        consume(kj, sa_scr, False)
        scores(kj + 2, sa_scr)
        consume(kj + 1, sb_scr, False)
        return carry

    lax.fori_loop(0, qi // 2, body, 0)

    @pl.when(qi % 2 == 0)
    def _():
        consume(qi, sa_scr, True)

    @pl.when(qi % 2 == 1)
    def _():
        scores(qi, sb_scr)
        consume(qi - 1, sa_scr, False)
        consume(qi, sb_scr, True)

    lam = _lam_value(lq1_ref[...], lk1_ref[...], lq2_ref[...], lk2_ref[...], lam_init)
    o = acc_scr[...] / jnp.sum(l_scr[...], axis=0, keepdims=True)
    o = (o[:, :tq] - lam * o[:, tq:]).T
    y_ref[...] = _rms(o) * an_ref[...] * (1.0 - lam_init)


def _attn_prompt(lam_params, at_norm, zq, k, v, n_batch, seq, n_heads, lam_init, tile=512):
    m = k.shape[0]
    nt = seq // tile
    qk_dim = lam_params[0].shape[1]
    small = [pl.BlockSpec(p.shape, lambda b, h, i: (0, 0)) for p in lam_params]
    kv_spec = pl.BlockSpec((seq, LANES), lambda b, h, i: (b, h))

    return pl.pallas_call(
        functools.partial(_attn_prompt_kernel, lam_init=lam_init, tile=tile, qk_dim=qk_dim),
        grid=(n_batch, n_heads, nt),
        in_specs=small + [pl.BlockSpec((1, LANES), lambda b, h, i: (0, 0)),
                          pl.BlockSpec((None, tile, LANES), lambda b, h, i: (4, b * nt + i, h)),
                          kv_spec, kv_spec],
        out_specs=pl.BlockSpec((tile, LANES), lambda b, h, i: (b * nt + i, h)),
        out_shape=jax.ShapeDtypeStruct((m, n_heads * LANES), F32),
        scratch_shapes=[pltpu.VMEM((seq, LANES), BF16), pltpu.VMEM((nt, LANES, tile), BF16),
                        pltpu.VMEM((2 * tile, LANES), BF16),
                        pltpu.VMEM((tile, 2 * tile), F32), pltpu.VMEM((tile, 2 * tile), F32),
                        pltpu.VMEM((1, 2 * tile), F32),
                        pltpu.VMEM((8, 2 * tile), F32), pltpu.VMEM((LANES, 2 * tile), F32)],
        compiler_params=_params("arbitrary", "arbitrary", "arbitrary"),
        name="attn_prompt",
    )(*lam_params, at_norm, zq, k, v)


def _attn_sample_kernel(pt_ref, lq1_ref, lk1_ref, lq2_ref, lk2_ref, an_ref, q_ref, kn_ref, vn_ref,
                        *rest, lam_init, n_heads, qk_dim, pages):
    k_refs = rest[:pages]
    v_refs = rest[pages:2 * pages]
    y_ref, m_scr, l_scr, acc_scr = rest[2 * pages:]
    j = pl.program_id(1)
    page_rows = k_refs[0].shape[0] * n_heads

    q = q_ref[...] * (qk_dim ** -0.5)
    lane = lax.broadcasted_iota(jnp.int32, (1, LANES), 1)
    qm = jnp.concatenate([jnp.where(lane < qk_dim, q, 0.0), jnp.where(lane >= qk_dim, q, 0.0)], axis=0)

    @pl.when(j == 0)
    def _():
        kn = kn_ref[...]
        vn = vn_ref[...]
        m_scr[...] = jnp.sum(qm * jnp.concatenate([kn, kn], axis=0), axis=-1, keepdims=True)
        l_scr[...] = jnp.ones_like(l_scr)
        acc_scr[...] = jnp.concatenate([vn, vn], axis=0)

    qm16 = qm.astype(BF16)
    own_head = (lax.broadcasted_iota(jnp.int32, (2 * n_heads, page_rows), 1) % n_heads
                == lax.broadcasted_iota(jnp.int32, (2 * n_heads, page_rows), 0) % n_heads)
    scores = [jnp.where(own_head, _dot_nt(qm16, k_refs[p][...].reshape(page_rows, LANES).astype(BF16)),
                        -jnp.inf) for p in range(pages)]
    m_old = m_scr[...]
    m_new = functools.reduce(jnp.maximum, [jnp.max(s, axis=-1, keepdims=True) for s in scores] + [m_old])
    corr = jnp.exp(m_old - m_new)
    l_new = l_scr[...] * corr
    acc = acc_scr[...] * corr
    for p in range(pages):
        pr = jnp.exp(scores[p] - m_new)
        l_new = l_new + jnp.sum(pr, axis=-1, keepdims=True)
        acc = acc + _dot(pr.astype(BF16), v_refs[p][...].reshape(page_rows, LANES).astype(BF16))
    l_scr[...] = l_new
    acc_scr[...] = acc
    m_scr[...] = m_new

    @pl.when(j == pl.num_programs(1) - 1)
    def _():
        lam = _lam_value(lq1_ref[...], lk1_ref[...], lq2_ref[...], lk2_ref[...], lam_init)
        o = acc_scr[...] / l_scr[...]
        w = o[:n_heads] - lam * o[n_heads:]
        y_ref[...] = _rms(w) * an_ref[...] * (1.0 - lam_init)


def _attn_sample(page_table, lam_params, at_norm, q3, kn3, vn3, cache_k, cache_v, lam_init, pages=8):
    n_seq, n_pages = page_table.shape
    _, _, page_size, n_heads, _ = cache_k.shape
    qk_dim = lam_params[0].shape[1]
    assert n_pages % pages == 0
    small = [pl.BlockSpec(p.shape, lambda b, j, pt: (0, 0)) for p in lam_params]
    row_spec = pl.BlockSpec((None, n_heads, LANES), lambda b, j, pt: (b, 0, 0))

    def page_spec(p):
        return pl.BlockSpec(
            (None, None, page_size, n_heads, LANES),
            lambda b, j, pt: (0, pt[b * n_pages + j * pages + p], 0, 0, 0))

    grid_spec = pltpu.PrefetchScalarGridSpec(
        num_scalar_prefetch=1,
        grid=(n_seq, n_pages // pages),
        in_specs=(small + [pl.BlockSpec((1, LANES), lambda b, j, pt: (0, 0)), row_spec, row_spec, row_spec]
                  + [page_spec(p) for p in range(pages)] + [page_spec(p) for p in range(pages)]),
        out_specs=row_spec,
        scratch_shapes=[pltpu.VMEM((2 * n_heads, 1), F32), pltpu.VMEM((2 * n_heads, 1), F32),
                        pltpu.VMEM((2 * n_heads, LANES), F32)],
    )
    return pl.pallas_call(
        functools.partial(_attn_sample_kernel, lam_init=lam_init, n_heads=n_heads, qk_dim=qk_dim,
                          pages=pages),
        grid_spec=grid_spec,
        out_shape=jax.ShapeDtypeStruct((n_seq, n_heads, LANES), F32),
        compiler_params=_params("arbitrary", "arbitrary"),
        name="attn_sample",
    )(page_table.reshape(-1), *lam_params, at_norm, q3, kn3, vn3,
      *([cache_k] * pages), *([cache_v] * pages))


def _outproj_kernel(x_ref, yh_ref, ya_ref, g1_ref, sh_ref, sc_ref, n2_ref, w_ref, o_ref, h_ref):
    hw = yh_ref.shape[1]
    rows = min(256, x_ref.shape[0])
    for r in range(x_ref.shape[0] // rows):
        rs = slice(r * rows, (r + 1) * rows)
        acc = (_dot(yh_ref[rs, :].astype(BF16), w_ref[:hw, :])
               + _dot(ya_ref[rs, :].astype(BF16), w_ref[hw:, :]))
        x1 = x_ref[rs, :] + g1_ref[...] * acc
        o_ref[rs, :] = x1
        h_ref[rs, :] = (_rms(x1) * n2_ref[...] * (1.0 + sc_ref[...]) + sh_ref[...]).astype(BF16)


def _outproj(x2, y_hg, y_at, mod, mod_spec, norm2, w_out, tm):
    m, d = x2.shape
    return pl.pallas_call(
        _outproj_kernel,
        grid=(m // tm,),
        in_specs=[pl.BlockSpec((tm, d), lambda i: (i, 0)),
                  pl.BlockSpec((tm, y_hg.shape[1]), lambda i: (i, 0)),
                  pl.BlockSpec((tm, y_at.shape[1]), lambda i: (i, 0)),
                  mod_spec(2), mod_spec(3), mod_spec(4),
                  pl.BlockSpec((1, d), lambda i: (0, 0)),
                  pl.BlockSpec(w_out.shape, lambda i: (0, 0))],
        out_specs=[pl.BlockSpec((tm, d), lambda i: (i, 0)),
                   pl.BlockSpec((tm, d), lambda i: (i, 0))],
        out_shape=[jax.ShapeDtypeStruct((m, d), F32),
                   jax.ShapeDtypeStruct((m, d), BF16)],
        compiler_params=_params("arbitrary"),
        name="outproj",
    )(x2, y_hg, y_at, mod, mod, mod, norm2, w_out)


def _conv_gate(conv, u):
    return (_silu(conv) * u).astype(BF16)


def _ffn_up_kernel(h_ref, halo_ref, wa_ref, wu_ref, cw_ref, cb_ref, g_ref, tail_ref, h_scr, a_scr,
                   *, tm, tiles_per_seq, sub_blocks):
    i = pl.program_id(0)

    @pl.when(pl.program_id(1) == 0)
    def _():
        h_scr[:HALO, :] = halo_ref[...]
        h_scr[HALO:, :] = h_ref[...]

    rows = tm // sub_blocks
    for r in range(sub_blocks):
        lo = HALO + r * rows
        if r == 0:
            a_scr[:lo + rows, :] = _dot(h_scr[:lo + rows, :], wa_ref[...])

            @pl.when(i % tiles_per_seq == 0)
            def _():
                a_scr[:HALO, :] = jnp.zeros((HALO, a_scr.shape[1]), F32)
        else:
            a_scr[lo:lo + rows, :] = _dot(h_scr[lo:lo + rows, :], wa_ref[...])
        u = _dot(h_scr[lo:lo + rows, :], wu_ref[...])
        conv = (cb_ref[...] + a_scr[lo - 2:lo - 2 + rows, :] * cw_ref[0:1, :]
                + a_scr[lo - 1:lo - 1 + rows, :] * cw_ref[1:2, :]
                + a_scr[lo:lo + rows, :] * cw_ref[2:3, :])
        g_ref[r * rows:(r + 1) * rows, :] = _conv_gate(conv, u)
    tail_ref[...] = a_scr[HALO + tm - 2:, :]


def _ffn_up(h2, w_up, conv_w, conv_b, seq, tm=1024, tf=512, sub_blocks=4):
    m, d = h2.shape
    ff = w_up.shape[1] // 2
    nf = ff // tf
    tm = min(tm, seq)
    tps = seq // tm
    assert conv_w.shape[0] == 3 and ff % tf == 0 and seq % tm == 0
    g, tails = pl.pallas_call(
        functools.partial(_ffn_up_kernel, tm=tm, tiles_per_seq=tps, sub_blocks=sub_blocks),
        grid=(m // tm, nf),
        in_specs=[pl.BlockSpec((tm, d), lambda i, f: (i, 0)),
                  pl.BlockSpec((HALO, d), lambda i, f: (jnp.maximum(i * (tm // HALO) - 1, 0), 0)),
                  pl.BlockSpec((d, tf), lambda i, f: (0, f)),
                  pl.BlockSpec((d, tf), lambda i, f: (0, nf + f)),
                  pl.BlockSpec((3, tf), lambda i, f: (0, f)),
                  pl.BlockSpec((1, tf), lambda i, f: (0, f))],
        out_specs=[pl.BlockSpec((tm, tf), lambda i, f: (i, f)),
                   pl.BlockSpec((None, 2, tf), lambda i, f: (i, 0, f))],
        out_shape=[jax.ShapeDtypeStruct((m, ff), BF16),
                   jax.ShapeDtypeStruct((m // tm, 2, ff), F32)],
        scratch_shapes=[pltpu.VMEM((HALO + tm, d), BF16), pltpu.VMEM((HALO + tm, tf), F32)],
        compiler_params=_params("arbitrary", "arbitrary"),
        name="ffn_up",
    )(h2, h2, w_up, w_up, conv_w, conv_b)
    return g, tails[tps - 1::tps]


def _ffn_down_kernel(g_ref, x_ref, g2_ref, wd_ref, fn_ref, y_ref, ss_scr, *, tn, sub_blocks):
    n = pl.program_id(1)
    d = y_ref.shape[1]

    @pl.when(n == 0)
    def _():
        ss_scr[...] = jnp.zeros_like(ss_scr)

    rows = y_ref.shape[0] // sub_blocks
    for r in range(sub_blocks):
        rs = slice(r * rows, (r + 1) * rows)
        x2 = x_ref[rs, :] + g2_ref[...] * _dot(g_ref[rs, :], wd_ref[...])
        ss_scr[rs, :] += jnp.sum(x2 * x2, axis=-1, keepdims=True)
        y_ref[rs, pl.ds(pl.multiple_of(n * tn, tn), tn)] = x2

    @pl.when(n == d // tn - 1)
    def _():
        y_ref[...] = y_ref[...] * lax.rsqrt(ss_scr[...] / d + EPS) * fn_ref[...]


def _ffn_down(g, x1, mod_p, w_down, final_norm, seq, tm=1024, tn=256, sub_blocks=4):
    m, d = x1.shape
    ff = w_down.shape[0]
    tm = min(tm, seq)
    tps = seq // tm
    nd = d // tn
    assert d % tn == 0 and seq % tm == 0
    return pl.pallas_call(
        functools.partial(_ffn_down_kernel, tn=tn, sub_blocks=sub_blocks),
        grid=(m // tm, nd),
        in_specs=[pl.BlockSpec((tm, ff), lambda i, n: (i, 0)),
                  pl.BlockSpec((tm, tn), lambda i, n: (i, n)),
                  pl.BlockSpec((None, 1, tn), lambda i, n: (i // tps, 0, 5 * nd + n)),
                  pl.BlockSpec((ff, tn), lambda i, n: (0, n)),
                  pl.BlockSpec((1, d), lambda i, n: (0, 0))],
        out_specs=pl.BlockSpec((tm, d), lambda i, n: (i, 0)),
        out_shape=jax.ShapeDtypeStruct((m, d), F32),
        scratch_shapes=[pltpu.VMEM((tm, 1), F32)],
        compiler_params=_params("arbitrary", "arbitrary"),
        name="ffn_down",
    )(g, x1, mod_p, w_down, final_norm)


def _ffn_sample_kernel(x_ref, h_ref, g2_ref, p0_ref, p1_ref, wa_ref, wu_ref, cw_ref,
                       cb_ref, wd_ref, fn_ref, y_ref, a_ref, acc_scr):
    f = pl.program_id(0)

    @pl.when(f == 0)
    def _():
        acc_scr[...] = jnp.zeros_like(acc_scr)

    a = _dot(h_ref[...], wa_ref[...])
    u = _dot(h_ref[...], wu_ref[...])
    conv = (cb_ref[...] + p0_ref[...] * cw_ref[0:1, :] + p1_ref[...] * cw_ref[1:2, :]
            + a * cw_ref[2:3, :])
    acc_scr[...] += _dot(_conv_gate(conv, u), wd_ref[...])
    a_ref[...] = a

    @pl.when(f == pl.num_programs(0) - 1)
    def _():
        y_ref[...] = _rms(x_ref[...] + g2_ref[...] * acc_scr[...]) * fn_ref[...]


def _ffn_sample(x1, h2, mod_s, prev2, w_up, conv_w, conv_b, w_down, final_norm, tf=512):
    m, d = x1.shape
    ff = w_down.shape[0]
    nf = ff // tf

    return pl.pallas_call(
        _ffn_sample_kernel,
        grid=(nf,),
        in_specs=[pl.BlockSpec((m, d), lambda f: (0, 0)),
                  pl.BlockSpec((m, d), lambda f: (0, 0)),
                  pl.BlockSpec((m, d), lambda f: (0, 5)),
                  pl.BlockSpec((m, tf), lambda f: (0, f)),
                  pl.BlockSpec((m, tf), lambda f: (0, nf + f)),
                  pl.BlockSpec((d, tf), lambda f: (0, f)),
                  pl.BlockSpec((d, tf), lambda f: (0, nf + f)),
                  pl.BlockSpec((3, tf), lambda f: (0, f)),
                  pl.BlockSpec((1, tf), lambda f: (0, f)),
                  pl.BlockSpec((tf, d), lambda f: (f, 0)),
                  pl.BlockSpec((1, d), lambda f: (0, 0))],
        out_specs=[pl.BlockSpec((m, d), lambda f: (0, 0)),
                   pl.BlockSpec((m, tf), lambda f: (0, f))],
        out_shape=[jax.ShapeDtypeStruct((m, d), F32),
                   jax.ShapeDtypeStruct((m, ff), F32)],
        scratch_shapes=[pltpu.VMEM((m, d), F32)],
        compiler_params=_params("arbitrary"),
        name="ffn_sample",
    )(x1, h2, mod_s, prev2, prev2, w_up, w_up, conv_w, conv_b, w_down, final_norm)


def _rope_tables(pos, qk_dim):
    rot = qk_dim // 4
    half = rot // 2
    inv_freq = ROPE_THETA ** (-jnp.arange(half, dtype=F32) / half)
    ang = pos.astype(F32)[:, None] * inv_freq[None, :]
    cos, sin = jnp.cos(ang), jnp.sin(ang)
    ones = jnp.ones((pos.shape[0], qk_dim - rot), F32)
    zeros = jnp.zeros((pos.shape[0], qk_dim - rot), F32)
    zh = jnp.zeros_like(sin)
    cos_t = jnp.concatenate([cos, cos, ones] * 2, axis=1)
    sin_a = jnp.concatenate([zh, sin, zeros] * 2, axis=1)
    sin_b = jnp.concatenate([-sin, zh, zeros] * 2, axis=1)
    return cos_t, sin_a, sin_b


def kernel(x_prompt, x_sample, c_prompt, c_sample, cache_k, cache_v, state_hgrn, state_conv, page_table,
           w_ada, b_ada, norm1, norm2, w_in, hg_lb_logits, hg_norm, lam_q1, lam_k1, lam_q2, lam_k2,
           at_norm, w_out, w_up, conv_w, conv_b, w_down, final_norm):
    n_batch, seq, d = x_prompt.shape
    n_seq = x_sample.shape[0]
    depth = w_in.shape[0]
    assert depth == 1 and x_sample.shape[1] == 1
    n_heads = state_hgrn.shape[2]
    assert cache_k.shape[3] == n_heads and cache_k.shape[4] == LANES and state_hgrn.shape[3] == LANES
    qk_dim = lam_q1.shape[1]
    past_len = page_table.shape[1] * cache_k.shape[2]
    lam_init = 0.8 - 0.6 * math.exp(-0.3 * 0)
    ff = w_down.shape[1]
    width = n_heads * LANES
    lam_params = [lam_q1, lam_k1, lam_q2, lam_k2]

    w_in16 = w_in[0].astype(BF16)
    w_out16 = w_out[0].astype(BF16)
    w_up16 = w_up[0].astype(BF16)
    w_down16 = w_down[0].astype(BF16)

    mod = _adaln(jnp.concatenate([c_prompt, c_sample], axis=0), w_ada[0], b_ada[0])
    mod_p = mod[:n_batch].reshape(n_batch, 1, N_MOD * d)
    mod_s = mod[n_batch:]

    tm_in = min(1024, seq)
    tps_in = seq // tm_in
    tm_out = min(512, seq)
    tps_out = seq // tm_out
    xp = x_prompt.reshape(n_batch * seq, d)
    tables_p = _rope_tables(jnp.arange(seq, dtype=jnp.int32), qk_dim)

    def pspec(part):
        return pl.BlockSpec((None, 1, d), lambda i, j: (i // tps_in, 0, part))

    zq, k_p, v_p = _inproj(xp, mod_p, mod_p, (pspec(0), pspec(1)), norm1, w_in16, tables_p,
                           pl.BlockSpec((tm_in, LANES), lambda i, j: (i % tps_in, 0)), tm_in, n_heads)
    y_hg, hgrn_p = _hgrn_prompt(zq, hg_lb_logits, hg_norm, n_batch, seq, n_heads)
    y_at = _attn_prompt(lam_params, at_norm, zq, k_p, v_p, n_batch, seq, n_heads, lam_init)
    x1, h2 = _outproj(xp, y_hg, y_at, mod_p,
                      lambda part: pl.BlockSpec((None, 1, d), lambda i: (i // tps_out, 0, part)),
                      norm2, w_out16, tm_out)
    g_p, conv_p = _ffn_up(h2, w_up16, conv_w[0], conv_b, seq)
    y_p = _ffn_down(g_p, x1, mod_p, w_down16, final_norm.reshape(1, d), seq)

    xs = x_sample.reshape(n_seq, d)
    tables_s = _rope_tables(jnp.full((n_seq,), past_len, jnp.int32), qk_dim)

    def sspec(part):
        return pl.BlockSpec((n_seq, d), lambda i, j: (0, part))

    zq_s, k_s, v_s = _inproj(xs, mod_s, mod_s, (sspec(0), sspec(1)), norm1, w_in16, tables_s,
                             pl.BlockSpec((n_seq, LANES), lambda i, j: (0, 0)), n_seq, n_heads, split=1)
    zq_s4 = zq_s.reshape(5, n_seq, n_heads, LANES)
    y_hg_s, hgrn_s = _hgrn_sample(zq_s4, hg_lb_logits.reshape(-1, n_heads, LANES), hg_norm, state_hgrn[0])
    y_at_s = _attn_sample(page_table, lam_params, at_norm, zq_s4[4], k_s.reshape(n_seq, n_heads, LANES),
                          v_s.reshape(n_seq, n_heads, LANES), cache_k, cache_v, lam_init)
    x1_s, h2_s = _outproj(xs, y_hg_s.reshape(n_seq, width), y_at_s.reshape(n_seq, width), mod_s,
                          lambda part: pl.BlockSpec((n_seq, d), lambda i: (0, part)),
                          norm2, w_out16, n_seq)
    prev = state_conv[0]
    y_s, a_s = _ffn_sample(x1_s, h2_s, mod_s, prev.reshape(n_seq, 2 * ff), w_up16, conv_w[0], conv_b,
                           w_down16, final_norm.reshape(1, d))
    conv_s = jnp.stack([prev[:, 1, :], a_s], axis=1)

    return (y_p.reshape(n_batch, seq, d), y_s.reshape(n_seq, 1, d),
            k_p.reshape(1, n_batch, seq, n_heads, LANES), v_p.reshape(1, n_batch, seq, n_heads, LANES),
            k_s.reshape(1, n_seq, 1, n_heads, LANES), v_s.reshape(1, n_seq, 1, n_heads, LANES),
            hgrn_p[None], hgrn_s[None], conv_p[None], conv_s[None])
```

```python
import functools
import math

import jax
import jax.numpy as jnp
from jax import lax
from jax.experimental import pallas as pl
from jax.experimental.pallas import tpu as pltpu

EPS = 1e-6
ROPE_THETA = 500000.0
LOG2_E = math.log2(math.e)
N_MOD = 6
LANES = 128
HGRN_CHUNK = 64
HGRN_SUB = 16
HGRN_TILE = 8
HALO = 16
VMEM_LIMIT_BYTES = 56 * 1024 * 1024

F32 = jnp.float32
BF16 = jnp.bfloat16


def _params(*sem):
    return pltpu.CompilerParams(dimension_semantics=sem, vmem_limit_bytes=VMEM_LIMIT_BYTES)


def _silu(x):
    return x / (1.0 + jnp.exp(-x))


def _sigmoid(x):
    return 1.0 / (1.0 + jnp.exp(-x))


def _rms(x):
    return x * lax.rsqrt(jnp.mean(x * x, axis=-1, keepdims=True) + EPS)


def _dot(a, b):
    return jnp.dot(a, b, preferred_element_type=F32)


def _dot_nt(a, b):
    return lax.dot_general(a, b, (((1,), (1,)), ((), ())), preferred_element_type=F32)


def _lam_value(lq1, lk1, lq2, lk2, lam_init):
    return (jnp.exp(jnp.sum(lq1 * lk1, axis=-1, keepdims=True))
            - jnp.exp(jnp.sum(lq2 * lk2, axis=-1, keepdims=True)) + lam_init)


def _lower_bound(layer_logits):
    mx = functools.reduce(jnp.maximum, layer_logits)
    es = [jnp.exp(r - mx) for r in layer_logits]
    return es[0] / functools.reduce(jnp.add, es)


def _adaln_kernel(c_ref, w_ref, b_ref, o_ref):
    s = _silu(c_ref[...]).astype(BF16)
    o_ref[...] = _dot(s, w_ref[...].astype(BF16)) + b_ref[...]


def _adaln(c_all, w_ada, b_ada, tn=512):
    rows, d = c_all.shape
    n = w_ada.shape[1]
    assert n % tn == 0
    return pl.pallas_call(
        _adaln_kernel,
        grid=(n // tn,),
        in_specs=[pl.BlockSpec((rows, d), lambda j: (0, 0)),
                  pl.BlockSpec((d, tn), lambda j: (0, j)),
                  pl.BlockSpec((1, tn), lambda j: (0, j))],
        out_specs=pl.BlockSpec((rows, tn), lambda j: (0, j)),
        out_shape=jax.ShapeDtypeStruct((rows, n), F32),
        compiler_params=_params("arbitrary"),
        name="adaln",
    )(c_all, w_ada, b_ada.reshape(1, n))


def _rope(blk, cos_t, sin_a, sin_b):
    return (blk * cos_t + pltpu.roll(blk, 8, axis=1) * sin_a
            + pltpu.roll(blk, LANES - 8, axis=1) * sin_b)


def _inproj_kernel(x_ref, sh_ref, sc_ref, g_ref, w_ref, cos_ref, sa_ref, sb_ref,
                   zq_ref, k_ref, v_ref, h_scr, *, split, sub_blocks):
    j = pl.program_id(1)
    part = j // split

    tm, tn = zq_ref.shape
    rows = tm // sub_blocks

    def rows_of(ref, rs):
        return ref[...] if ref.shape[0] == 1 else ref[rs, :]

    def project(dst, rotary=False, modulate=False):
        for r in range(sub_blocks):
            rs = slice(r * rows, (r + 1) * rows)
            if modulate:
                h = _rms(x_ref[rs, :]) * g_ref[...] * (1.0 + rows_of(sc_ref, rs)) + rows_of(sh_ref, rs)
                h_scr[rs, :] = h.astype(BF16)
            acc = _dot(h_scr[rs, :], w_ref[...])
            if rotary:
                for hh in range(tn // LANES):
                    sl = slice(hh * LANES, (hh + 1) * LANES)
                    dst[rs, sl] = _rope(acc[:, sl], cos_ref[rs, :], sa_ref[rs, :], sb_ref[rs, :])
            else:
                dst[rs, :] = acc

    @pl.when(j == 0)
    def _():
        project(zq_ref, modulate=True)

    @pl.when((j > 0) & (part < 4))
    def _():
        project(zq_ref)

    @pl.when(part == 4)
    def _():
        project(zq_ref, rotary=True)

    @pl.when(part == 5)
    def _():
        project(k_ref, rotary=True)

    @pl.when(part == 6)
    def _():
        project(v_ref)


def _inproj(x2, sh, sc, mod_spec, norm_g, w_in, tables, tab_spec, tm, n_heads, split=2):
    m, d = x2.shape
    width = n_heads * LANES
    assert w_in.shape[1] == 7 * width and n_heads % split == 0
    tn = width // split
    sh_spec, sc_spec = mod_spec

    def zq_map(i, j):
        jz = jnp.minimum(j, 5 * split - 1)
        return (jz // split, i, jz % split)

    return pl.pallas_call(
        functools.partial(_inproj_kernel, split=split, sub_blocks=max(1, tm // 256)),
        grid=(m // tm, 7 * split),
        in_specs=[pl.BlockSpec((tm, d), lambda i, j: (i, 0)),
                  sh_spec, sc_spec,
                  pl.BlockSpec((1, d), lambda i, j: (0, 0)),
                  pl.BlockSpec((d, tn), lambda i, j: (0, j)),
                  tab_spec, tab_spec, tab_spec],
        out_specs=[pl.BlockSpec((None, tm, tn), zq_map),
                   pl.BlockSpec((tm, tn), lambda i, j: (i, jnp.clip(j - 5 * split, 0, split - 1))),
                   pl.BlockSpec((tm, tn), lambda i, j: (i, jnp.clip(j - 6 * split, 0, split - 1)))],
        out_shape=[jax.ShapeDtypeStruct((5, m, width), F32),
                   jax.ShapeDtypeStruct((m, width), F32),
                   jax.ShapeDtypeStruct((m, width), F32)],
        scratch_shapes=[pltpu.VMEM((tm, d), BF16)],
        compiler_params=_params("arbitrary", "arbitrary"),
        name="inproj",
    )(x2, sh, sc, norm_g, w_in, *tables)


def _roll_in_tiles(x, shift):
    if shift == 0:
        return x
    tiles = [pltpu.roll(x[r:r + HGRN_TILE, :], shift, axis=0) for r in range(0, x.shape[0], HGRN_TILE)]
    return jnp.concatenate(tiles, axis=0)


def _hgrn_masks():
    c_rows = HGRN_CHUNK
    ri = lax.broadcasted_iota(jnp.int32, (c_rows, c_rows), 0)
    ci = lax.broadcasted_iota(jnp.int32, (c_rows, c_rows), 1)
    tri = jnp.where(ri >= ci, 1.0, 0.0).astype(BF16)
    same_sub = ri // HGRN_SUB == ci // HGRN_SUB
    same_tile = ri // HGRN_TILE == ci // HGRN_TILE
    diag = [same_tile & (ci == ri - dlt) for dlt in range(HGRN_TILE)]
    row = lax.broadcasted_iota(jnp.int32, (c_rows, 1), 0)
    return tri, same_sub, diag, row


def _hgrn_chunk(q, hf, v, g, lb, gn, st, masks):
    c_rows, sub, tile = HGRN_CHUNK, HGRN_SUB, HGRN_TILE
    tri, same_sub, diag, row = masks
    f = lb + (1.0 - lb) * _sigmoid(hf)
    kk = 1.0 - f
    lf = jnp.log(f)
    hi = lf.astype(BF16)
    r1 = lf - hi.astype(F32)
    mid = r1.astype(BF16)
    lo = (r1 - mid.astype(F32)).astype(BF16)
    b = _dot(tri, hi) + _dot(tri, mid) + _dot(tri, lo)
    b_last = b[c_rows - 1:c_rows, :]

    o = _dot_nt((q * jnp.exp(b)).astype(BF16), st.astype(BF16))
    kd = kk * jnp.exp(b_last - b)
    st_new = st * jnp.exp(b_last) + _dot(v.T.astype(BF16), kd.astype(BF16))

    v16 = v.astype(BF16)
    for jb in range(c_rows // sub - 1):
        e = (jb + 1) * sub
        r_ref = b[e - 1:e, :]
        qj = q * jnp.exp(jnp.minimum(b - r_ref, 0.0))
        kj = kk[jb * sub:e, :] * jnp.exp(r_ref - b[jb * sub:e, :])
        att = _dot_nt(qj.astype(BF16), kj.astype(BF16))
        att = jnp.where(row >= e, att, 0.0)
        o = o + _dot(att.astype(BF16), v16[jb * sub:e, :])
    zero_tile = jnp.zeros((tile, LANES), F32)
    q_parts, k_parts = [], []
    for r0 in range(0, c_rows, sub):
        r_ref = b[r0 + tile - 1:r0 + tile, :]
        first, second = slice(r0, r0 + tile), slice(r0 + tile, r0 + sub)
        k_parts += [kk[first, :] * jnp.exp(r_ref - b[first, :]), zero_tile]
        q_parts += [zero_tile, q[second, :] * jnp.exp(b[second, :] - r_ref)]
    att = _dot_nt(jnp.concatenate(q_parts, axis=0).astype(BF16), jnp.concatenate(k_parts, axis=0).astype(BF16))
    att = jnp.where(same_sub, att, 0.0)
    decay = None
    for dlt in range(tile):
        if dlt == 0:
            w = q * kk
        else:
            gate = _roll_in_tiles(f, dlt - 1)
            decay = gate if decay is None else decay * gate
            w = q * _roll_in_tiles(kk, dlt) * decay
        att = att + jnp.where(diag[dlt], jnp.sum(w, axis=-1, keepdims=True), 0.0)
    o = o + _dot(att.astype(BF16), v16)
    return _rms(o) * gn * _silu(g), st_new


def _hgrn_prompt_step(q_ref, f_ref, v_ref, g_ref, lbl_ref, gn_ref, y_ref, st_scr, group):
    c_rows = HGRN_CHUNK
    lb = _lower_bound([lbl_ref[l:l + 1, :] for l in range(lbl_ref.shape[0])])
    masks = _hgrn_masks()
    for r0 in range(0, q_ref.shape[0], c_rows):
        rs = slice(r0, r0 + c_rows)
        for gi in range(group):
            sl = slice(gi * LANES, (gi + 1) * LANES)
            y, st_new = _hgrn_chunk(q_ref[rs, sl], f_ref[rs, sl], v_ref[rs, sl], g_ref[rs, sl],
                                    lb[:, sl], gn_ref[...], st_scr[gi], masks)
            st_scr[gi] = st_new
            y_ref[rs, sl] = y


def _hgrn_sample_kernel(q_ref, f_ref, v_ref, g_ref, lbl_ref, gn_ref, s_ref, y_ref, so_ref,
                        *, n_heads):
    lb = _lower_bound([lbl_ref[l] for l in range(lbl_ref.shape[0])])
    pad = jnp.zeros((LANES - 3 * n_heads, LANES), F32)
    for s in range(q_ref.shape[0]):
        q = q_ref[s]
        v = v_ref[s]
        f = lb + (1.0 - lb) * _sigmoid(f_ref[s])
        kk = 1.0 - f
        cols = jnp.concatenate([f, kk, q, pad], axis=0).T
        o_rows = []
        for h in range(n_heads):
            f_col = cols[:, h:h + 1]
            k_col = cols[:, n_heads + h:n_heads + h + 1]
            q_col = cols[:, 2 * n_heads + h:2 * n_heads + h + 1]
            s_new = f_col * s_ref[s, h] + k_col * v[h:h + 1, :]
            so_ref[s, h] = s_new
            o_rows.append(jnp.sum(q_col * s_new, axis=0, keepdims=True))
        o = jnp.concatenate(o_rows, axis=0)
        y_ref[s] = _rms(o) * gn_ref[...] * _silu(g_ref[s])


def _hgrn_sample(zq4, lb_logits3, hg_norm, state, per_step=4):
    n_seq, n_heads = state.shape[0], state.shape[1]
    per_step = math.gcd(per_step, n_seq)

    def zspec(part):
        return pl.BlockSpec((None, per_step, n_heads, LANES), lambda b: (part, b, 0, 0))

    return pl.pallas_call(
        functools.partial(_hgrn_sample_kernel, n_heads=n_heads),
        grid=(n_seq // per_step,),
        in_specs=[zspec(0), zspec(1), zspec(2), zspec(3),
                  pl.BlockSpec(lb_logits3.shape, lambda b: (0, 0, 0)),
                  pl.BlockSpec((1, LANES), lambda b: (0, 0)),
                  pl.BlockSpec((per_step, n_heads, LANES, LANES), lambda b: (b, 0, 0, 0))],
        out_specs=[pl.BlockSpec((per_step, n_heads, LANES), lambda b: (b, 0, 0)),
                   pl.BlockSpec((per_step, n_heads, LANES, LANES), lambda b: (b, 0, 0, 0))],
        out_shape=[jax.ShapeDtypeStruct((n_seq, n_heads, LANES), F32),
                   jax.ShapeDtypeStruct(state.shape, F32)],
        compiler_params=_params("arbitrary"),
        name="hgrn_sample",
    )(zq4, zq4, zq4, zq4, lb_logits3, hg_norm, state)


def _attn_prompt_kernel(lq1_ref, lk1_ref, lq2_ref, lk2_ref, an_ref, q_ref, k_ref, v_ref, y_ref,
                        k16_scr, vt_scr, qm_scr, sa_scr, sb_scr, m_scr, l_scr, acc_scr, *, lam_init, tile,
                        qk_dim):
    qi = pl.program_id(2)
    n_kv = vt_scr.shape[0]
    tq = tk = tile

    @pl.when(qi == 0)
    def _():
        k16_scr[...] = k_ref[...].astype(BF16)
        for t in range(n_kv):
            vt_scr[t] = v_ref[t * tk:(t + 1) * tk, :].T.astype(BF16)

    q = q_ref[...] * (qk_dim ** -0.5 * LOG2_E)
    lane = lax.broadcasted_iota(jnp.int32, (1, LANES), 1)
    qm_scr[:tq, :] = jnp.where(lane < qk_dim, q, 0.0).astype(BF16)
    qm_scr[tq:, :] = jnp.where(lane >= qk_dim, q, 0.0).astype(BF16)
    m_scr[...] = jnp.full_like(m_scr, -jnp.inf)
    l_scr[...] = jnp.zeros_like(l_scr)
    acc_scr[...] = jnp.zeros_like(acc_scr)

    def scores(kj, dst):
        r0 = pl.multiple_of(kj * tk, tk)
        dst[...] = _dot_nt(k16_scr[pl.ds(r0, tk), :], qm_scr[...])

    def consume(kj, src, on_diagonal):
        st = src[...]
        if on_diagonal:
            key = lax.broadcasted_iota(jnp.int32, (tk, 2 * tq), 0)
            qry = lax.broadcasted_iota(jnp.int32, (tk, 2 * tq), 1) % tq
            st = jnp.where(key <= qry, st, -jnp.inf)
        m_old = m_scr[...]
        m_new = jnp.maximum(m_old, jnp.max(st, axis=0, keepdims=True))
        p = jnp.exp2(st - m_new)
        corr = jnp.exp2(m_old - m_new)
        l_scr[...] = l_scr[...] * corr + jnp.sum(p.reshape(tk // 8, 8, 2 * tq), axis=0)
        acc_scr[...] = acc_scr[...] * corr + _dot(vt_scr[kj], p.astype(BF16))
        m_scr[...] = m_new

    scores(0, sa_scr)

    def body(pair, carry):
        kj = 2 * pair
        scores(kj + 1, sb_scr)
        consume(kj, sa_scr, False)
        scores(kj + 2, sa_scr)
        consume(kj + 1, sb_scr, False)
        return carry

    lax.fori_loop(0, qi // 2, body, 0)

    @pl.when(qi % 2 == 0)
    def _():
        consume(qi, sa_scr, True)

    @pl.when(qi % 2 == 1)
    def _():
        scores(qi, sb_scr)
        consume(qi - 1, sa_scr, False)
        consume(qi, sb_scr, True)

    lam = _lam_value(lq1_ref[...], lk1_ref[...], lq2_ref[...], lk2_ref[...], lam_init)
    o = acc_scr[...] / jnp.sum(l_scr[...], axis=0, keepdims=True)
    o = (o[:, :tq] - lam * o[:, tq:]).T
    y_ref[...] = _rms(o) * an_ref[...] * (1.0 - lam_init)


def _attn_prompt(lam_params, at_norm, zq, k, v, n_batch, seq, n_heads, lam_init, tile=512):
    m = k.shape[0]
    nt = seq // tile
    qk_dim = lam_params[0].shape[1]
    small = [pl.BlockSpec(p.shape, lambda b, h, i: (0, 0)) for p in lam_params]
    kv_spec = pl.BlockSpec((seq, LANES), lambda b, h, i: (b, h))

    return pl.pallas_call(
        functools.partial(_attn_prompt_kernel, lam_init=lam_init, tile=tile, qk_dim=qk_dim),
        grid=(n_batch, n_heads, nt),
        in_specs=small + [pl.BlockSpec((1, LANES), lambda b, h, i: (0, 0)),
                          pl.BlockSpec((None, tile, LANES), lambda b, h, i: (4, b * nt + i, h)),
                          kv_spec, kv_spec],
        out_specs=pl.BlockSpec((tile, LANES), lambda b, h, i: (b * nt + i, h)),
        out_shape=jax.ShapeDtypeStruct((m, n_heads * LANES), F32),
        scratch_shapes=[pltpu.VMEM((seq, LANES), BF16), pltpu.VMEM((nt, LANES, tile), BF16),
                        pltpu.VMEM((2 * tile, LANES), BF16),
                        pltpu.VMEM((tile, 2 * tile), F32), pltpu.VMEM((tile, 2 * tile), F32),
                        pltpu.VMEM((1, 2 * tile), F32),
                        pltpu.VMEM((8, 2 * tile), F32), pltpu.VMEM((LANES, 2 * tile), F32)],
        compiler_params=_params("arbitrary", "arbitrary", "arbitrary"),
        name="attn_prompt",
    )(*lam_params, at_norm, zq, k, v)


def _pages_hgrn_kernel(pt_ref, lq1_ref, lk1_ref, lq2_ref, lk2_ref, an_ref, q_ref, kn_ref, vn_ref,
                       hq_ref, hf_ref, hv_ref, hg_ref, lbl_ref, gn_ref, *rest,
                       lam_init, n_heads, qk_dim, pages, group, sweep_steps):
    k_refs = rest[:pages]
    v_refs = rest[pages:2 * pages]
    y_ref, yh_ref, sfin_ref, m_scr, l_scr, acc_scr, st_scr = rest[2 * pages:]
    j = pl.program_id(1)
    page_rows = k_refs[0].shape[0] * n_heads
    sweep_pos = (pl.program_id(0) * pl.num_programs(1) + j) % sweep_steps

    @pl.when(sweep_pos == 0)
    def _():
        st_scr[...] = jnp.zeros_like(st_scr)

    q = q_ref[...] * (qk_dim ** -0.5)
    lane = lax.broadcasted_iota(jnp.int32, (1, LANES), 1)
    qm = jnp.concatenate([jnp.where(lane < qk_dim, q, 0.0), jnp.where(lane >= qk_dim, q, 0.0)], axis=0)

    @pl.when(j == 0)
    def _():
        kn = kn_ref[...]
        vn = vn_ref[...]
        m_scr[...] = jnp.sum(qm * jnp.concatenate([kn, kn], axis=0), axis=-1, keepdims=True)
        l_scr[...] = jnp.ones_like(l_scr)
        acc_scr[...] = jnp.concatenate([vn, vn], axis=0)

    qm16 = qm.astype(BF16)
    own_head = (lax.broadcasted_iota(jnp.int32, (2 * n_heads, page_rows), 1) % n_heads
                == lax.broadcasted_iota(jnp.int32, (2 * n_heads, page_rows), 0) % n_heads)
    _hgrn_prompt_step(hq_ref, hf_ref, hv_ref, hg_ref, lbl_ref, gn_ref, yh_ref, st_scr, group)

    scores = [jnp.where(own_head, _dot_nt(qm16, k_refs[p][...].reshape(page_rows, LANES).astype(BF16)),
                        -jnp.inf) for p in range(pages)]
    m_old = m_scr[...]
    m_new = functools.reduce(jnp.maximum, [jnp.max(s, axis=-1, keepdims=True) for s in scores] + [m_old])
    corr = jnp.exp(m_old - m_new)
    l_new = l_scr[...] * corr
    acc = acc_scr[...] * corr
    for p in range(pages):
        pr = jnp.exp(scores[p] - m_new)
        l_new = l_new + jnp.sum(pr, axis=-1, keepdims=True)
        acc = acc + _dot(pr.astype(BF16), v_refs[p][...].reshape(page_rows, LANES).astype(BF16))
    l_scr[...] = l_new
    acc_scr[...] = acc
    m_scr[...] = m_new

    @pl.when(sweep_pos == sweep_steps - 1)
    def _():
        for gi in range(group):
            sfin_ref[gi] = st_scr[gi].T

    @pl.when(j == pl.num_programs(1) - 1)
    def _():
        lam = _lam_value(lq1_ref[...], lk1_ref[...], lq2_ref[...], lk2_ref[...], lam_init)
        o = acc_scr[...] / l_scr[...]
        w = o[:n_heads] - lam * o[n_heads:]
        y_ref[...] = _rms(w) * an_ref[...] * (1.0 - lam_init)


def _pages_and_hgrn(page_table, lam_params, at_norm, q3, kn3, vn3, cache_k, cache_v, lam_init,
                    zq, lb_logits, hg_norm, n_batch, seq, pages=8, group=4):
    n_seq, n_pages = page_table.shape
    _, _, page_size, n_heads, _ = cache_k.shape
    qk_dim = lam_params[0].shape[1]
    assert n_pages % pages == 0
    page_steps = n_pages // pages
    n_steps = n_seq * page_steps
    group = math.gcd(group, n_heads)
    n_groups = n_heads // group
    gw = group * LANES
    m = zq.shape[1]
    assert (n_batch * n_groups * seq) % (n_steps * HGRN_CHUNK) == 0
    rows = n_batch * n_groups * seq // n_steps
    assert seq % rows == 0
    sweep_steps = seq // rows

    small = [pl.BlockSpec(p.shape, lambda b, j, pt: (0, 0)) for p in lam_params]
    row_spec = pl.BlockSpec((None, n_heads, LANES), lambda b, j, pt: (b, 0, 0))

    def page_spec(p):
        return pl.BlockSpec(
            (None, None, page_size, n_heads, LANES),
            lambda b, j, pt: (0, pt[b * n_pages + j * pages + p], 0, 0, 0))

    def sweep(b, j):
        step = b * page_steps + j
        return step // (sweep_steps * n_groups), (step // sweep_steps) % n_groups, step % sweep_steps

    def rows_map(b, j, pt):
        sb, sg, sp = sweep(b, j)
        return (sb * sweep_steps + sp, sg)

    def zspec(part):
        return pl.BlockSpec((None, rows, gw), lambda b, j, pt: (part,) + rows_map(b, j, pt))

    grid_spec = pltpu.PrefetchScalarGridSpec(
        num_scalar_prefetch=1,
        grid=(n_seq, page_steps),
        in_specs=(small + [pl.BlockSpec((1, LANES), lambda b, j, pt: (0, 0)), row_spec, row_spec, row_spec]
                  + [zspec(0), zspec(1), zspec(2), zspec(3),
                     pl.BlockSpec((lb_logits.shape[0], gw), lambda b, j, pt: (0, sweep(b, j)[1])),
                     pl.BlockSpec((1, LANES), lambda b, j, pt: (0, 0))]
                  + [page_spec(p) for p in range(pages)] + [page_spec(p) for p in range(pages)]),
        out_specs=[row_spec,
                   pl.BlockSpec((rows, gw), rows_map),
                   pl.BlockSpec((None, group, LANES, LANES),
                                lambda b, j, pt: (sweep(b, j)[0], sweep(b, j)[1], 0, 0))],
        scratch_shapes=[pltpu.VMEM((2 * n_heads, 1), F32), pltpu.VMEM((2 * n_heads, 1), F32),
                        pltpu.VMEM((2 * n_heads, LANES), F32), pltpu.VMEM((group, LANES, LANES), F32)],
    )
    return pl.pallas_call(
        functools.partial(_pages_hgrn_kernel, lam_init=lam_init, n_heads=n_heads, qk_dim=qk_dim,
                          pages=pages, group=group, sweep_steps=sweep_steps),
        grid_spec=grid_spec,
        out_shape=[jax.ShapeDtypeStruct((n_seq, n_heads, LANES), F32),
                   jax.ShapeDtypeStruct((m, n_heads * LANES), F32),
                   jax.ShapeDtypeStruct((n_batch, n_heads, LANES, LANES), F32)],
        compiler_params=_params("arbitrary", "arbitrary"),
        name="pages_hgrn",
    )(page_table.reshape(-1), *lam_params, at_norm, q3, kn3, vn3, zq, zq, zq, zq, lb_logits, hg_norm,
      *([cache_k] * pages), *([cache_v] * pages))


def _outproj_kernel(x_ref, yh_ref, ya_ref, g1_ref, sh_ref, sc_ref, n2_ref, w_ref, o_ref, h_ref):
    hw = yh_ref.shape[1]
    rows = min(256, x_ref.shape[0])
    for r in range(x_ref.shape[0] // rows):
        rs = slice(r * rows, (r + 1) * rows)
        acc = (_dot(yh_ref[rs, :].astype(BF16), w_ref[:hw, :])
               + _dot(ya_ref[rs, :].astype(BF16), w_ref[hw:, :]))
        x1 = x_ref[rs, :] + g1_ref[...] * acc
        o_ref[rs, :] = x1
        h_ref[rs, :] = (_rms(x1) * n2_ref[...] * (1.0 + sc_ref[...]) + sh_ref[...]).astype(BF16)


def _outproj(x2, y_hg, y_at, mod, mod_spec, norm2, w_out, tm):
    m, d = x2.shape
    return pl.pallas_call(
        _outproj_kernel,
        grid=(m // tm,),
        in_specs=[pl.BlockSpec((tm, d), lambda i: (i, 0)),
                  pl.BlockSpec((tm, y_hg.shape[1]), lambda i: (i, 0)),
                  pl.BlockSpec((tm, y_at.shape[1]), lambda i: (i, 0)),
                  mod_spec(2), mod_spec(3), mod_spec(4),
                  pl.BlockSpec((1, d), lambda i: (0, 0)),
                  pl.BlockSpec(w_out.shape, lambda i: (0, 0))],
        out_specs=[pl.BlockSpec((tm, d), lambda i: (i, 0)),
                   pl.BlockSpec((tm, d), lambda i: (i, 0))],
        out_shape=[jax.ShapeDtypeStruct((m, d), F32),
                   jax.ShapeDtypeStruct((m, d), BF16)],
        compiler_params=_params("arbitrary"),
        name="outproj",
    )(x2, y_hg, y_at, mod, mod, mod, norm2, w_out)


def _conv_gate(conv, u):
    return (_silu(conv) * u).astype(BF16)


def _ffn_up_kernel(h_ref, halo_ref, wa_ref, wu_ref, cw_ref, cb_ref, g_ref, tail_ref, h_scr, a_scr,
                   *, tm, tiles_per_seq, sub_blocks):
    i = pl.program_id(0)

    @pl.when(pl.program_id(1) == 0)
    def _():
        h_scr[:HALO, :] = halo_ref[...]
        h_scr[HALO:, :] = h_ref[...]

    rows = tm // sub_blocks
    for r in range(sub_blocks):
        lo = HALO + r * rows
        if r == 0:
            a_scr[:lo + rows, :] = _dot(h_scr[:lo + rows, :], wa_ref[...])

            @pl.when(i % tiles_per_seq == 0)
            def _():
                a_scr[:HALO, :] = jnp.zeros((HALO, a_scr.shape[1]), F32)
        else:
            a_scr[lo:lo + rows, :] = _dot(h_scr[lo:lo + rows, :], wa_ref[...])
        u = _dot(h_scr[lo:lo + rows, :], wu_ref[...])
        conv = (cb_ref[...] + a_scr[lo - 2:lo - 2 + rows, :] * cw_ref[0:1, :]
                + a_scr[lo - 1:lo - 1 + rows, :] * cw_ref[1:2, :]
                + a_scr[lo:lo + rows, :] * cw_ref[2:3, :])
        g_ref[r * rows:(r + 1) * rows, :] = _conv_gate(conv, u)
    tail_ref[...] = a_scr[HALO + tm - 2:, :]


def _ffn_up(h2, w_up, conv_w, conv_b, seq, tm=1024, tf=512, sub_blocks=4):
    m, d = h2.shape
    ff = w_up.shape[1] // 2
    nf = ff // tf
    tm = min(tm, seq)
    tps = seq // tm
    assert conv_w.shape[0] == 3 and ff % tf == 0 and seq % tm == 0
    g, tails = pl.pallas_call(
        functools.partial(_ffn_up_kernel, tm=tm, tiles_per_seq=tps, sub_blocks=sub_blocks),
        grid=(m // tm, nf),
        in_specs=[pl.BlockSpec((tm, d), lambda i, f: (i, 0)),
                  pl.BlockSpec((HALO, d), lambda i, f: (jnp.maximum(i * (tm // HALO) - 1, 0), 0)),
                  pl.BlockSpec((d, tf), lambda i, f: (0, f)),
                  pl.BlockSpec((d, tf), lambda i, f: (0, nf + f)),
                  pl.BlockSpec((3, tf), lambda i, f: (0, f)),
                  pl.BlockSpec((1, tf), lambda i, f: (0, f))],
        out_specs=[pl.BlockSpec((tm, tf), lambda i, f: (i, f)),
                   pl.BlockSpec((None, 2, tf), lambda i, f: (i, 0, f))],
        out_shape=[jax.ShapeDtypeStruct((m, ff), BF16),
                   jax.ShapeDtypeStruct((m // tm, 2, ff), F32)],
        scratch_shapes=[pltpu.VMEM((HALO + tm, d), BF16), pltpu.VMEM((HALO + tm, tf), F32)],
        compiler_params=_params("arbitrary", "arbitrary"),
        name="ffn_up",
    )(h2, h2, w_up, w_up, conv_w, conv_b)
    return g, tails[tps - 1::tps]


def _ffn_down_kernel(g_ref, x_ref, g2_ref, wd_ref, fn_ref, y_ref, ss_scr, *, tn, sub_blocks):
    n = pl.program_id(1)
    d = y_ref.shape[1]

    @pl.when(n == 0)
    def _():
        ss_scr[...] = jnp.zeros_like(ss_scr)

    rows = y_ref.shape[0] // sub_blocks
    for r in range(sub_blocks):
        rs = slice(r * rows, (r + 1) * rows)
        x2 = x_ref[rs, :] + g2_ref[...] * _dot(g_ref[rs, :], wd_ref[...])
        ss_scr[rs, :] += jnp.sum(x2 * x2, axis=-1, keepdims=True)
        y_ref[rs, pl.ds(pl.multiple_of(n * tn, tn), tn)] = x2

    @pl.when(n == d // tn - 1)
    def _():
        y_ref[...] = y_ref[...] * lax.rsqrt(ss_scr[...] / d + EPS) * fn_ref[...]


def _ffn_down(g, x1, mod_p, w_down, final_norm, seq, tm=1024, tn=256, sub_blocks=4):
    m, d = x1.shape
    ff = w_down.shape[0]
    tm = min(tm, seq)
    tps = seq // tm
    nd = d // tn
    assert d % tn == 0 and seq % tm == 0
    return pl.pallas_call(
        functools.partial(_ffn_down_kernel, tn=tn, sub_blocks=sub_blocks),
        grid=(m // tm, nd),
        in_specs=[pl.BlockSpec((tm, ff), lambda i, n: (i, 0)),
                  pl.BlockSpec((tm, tn), lambda i, n: (i, n)),
                  pl.BlockSpec((None, 1, tn), lambda i, n: (i // tps, 0, 5 * nd + n)),
                  pl.BlockSpec((ff, tn), lambda i, n: (0, n)),
                  pl.BlockSpec((1, d), lambda i, n: (0, 0))],
        out_specs=pl.BlockSpec((tm, d), lambda i, n: (i, 0)),
        out_shape=jax.ShapeDtypeStruct((m, d), F32),
        scratch_shapes=[pltpu.VMEM((tm, 1), F32)],
        compiler_params=_params("arbitrary", "arbitrary"),
        name="ffn_down",
    )(g, x1, mod_p, w_down, final_norm)


def _ffn_sample_kernel(x_ref, h_ref, g2_ref, p0_ref, p1_ref, wa_ref, wu_ref, cw_ref,
                       cb_ref, wd_ref, fn_ref, y_ref, a_ref, acc_scr):
    f = pl.program_id(0)

    @pl.when(f == 0)
    def _():
        acc_scr[...] = jnp.zeros_like(acc_scr)

    a = _dot(h_ref[...], wa_ref[...])
    u = _dot(h_ref[...], wu_ref[...])
    conv = (cb_ref[...] + p0_ref[...] * cw_ref[0:1, :] + p1_ref[...] * cw_ref[1:2, :]
            + a * cw_ref[2:3, :])
    acc_scr[...] += _dot(_conv_gate(conv, u), wd_ref[...])
    a_ref[...] = a

    @pl.when(f == pl.num_programs(0) - 1)
    def _():
        y_ref[...] = _rms(x_ref[...] + g2_ref[...] * acc_scr[...]) * fn_ref[...]


def _ffn_sample(x1, h2, mod_s, prev2, w_up, conv_w, conv_b, w_down, final_norm, tf=512):
    m, d = x1.shape
    ff = w_down.shape[0]
    nf = ff // tf

    return pl.pallas_call(
        _ffn_sample_kernel,
        grid=(nf,),
        in_specs=[pl.BlockSpec((m, d), lambda f: (0, 0)),
                  pl.BlockSpec((m, d), lambda f: (0, 0)),
                  pl.BlockSpec((m, d), lambda f: (0, 5)),
                  pl.BlockSpec((m, tf), lambda f: (0, f)),
                  pl.BlockSpec((m, tf), lambda f: (0, nf + f)),
                  pl.BlockSpec((d, tf), lambda f: (0, f)),
                  pl.BlockSpec((d, tf), lambda f: (0, nf + f)),
                  pl.BlockSpec((3, tf), lambda f: (0, f)),
                  pl.BlockSpec((1, tf), lambda f: (0, f)),
                  pl.BlockSpec((tf, d), lambda f: (f, 0)),
                  pl.BlockSpec((1, d), lambda f: (0, 0))],
        out_specs=[pl.BlockSpec((m, d), lambda f: (0, 0)),
                   pl.BlockSpec((m, tf), lambda f: (0, f))],
        out_shape=[jax.ShapeDtypeStruct((m, d), F32),
                   jax.ShapeDtypeStruct((m, ff), F32)],
        scratch_shapes=[pltpu.VMEM((m, d), F32)],
        compiler_params=_params("arbitrary"),
        name="ffn_sample",
    )(x1, h2, mod_s, prev2, prev2, w_up, w_up, conv_w, conv_b, w_down, final_norm)


def _rope_tables(pos, qk_dim):
    rot = qk_dim // 4
    half = rot // 2
    inv_freq = ROPE_THETA ** (-jnp.arange(half, dtype=F32) / half)
    ang = pos.astype(F32)[:, None] * inv_freq[None, :]
    cos, sin = jnp.cos(ang), jnp.sin(ang)
    ones = jnp.ones((pos.shape[0], qk_dim - rot), F32)
    zeros = jnp.zeros((pos.shape[0], qk_dim - rot), F32)
    zh = jnp.zeros_like(sin)
    cos_t = jnp.concatenate([cos, cos, ones] * 2, axis=1)
    sin_a = jnp.concatenate([zh, sin, zeros] * 2, axis=1)
    sin_b = jnp.concatenate([-sin, zh, zeros] * 2, axis=1)
    return cos_t, sin_a, sin_b


def kernel(x_prompt, x_sample, c_prompt, c_sample, cache_k, cache_v, state_hgrn, state_conv, page_table,
           w_ada, b_ada, norm1, norm2, w_in, hg_lb_logits, hg_norm, lam_q1, lam_k1, lam_q2, lam_k2,
           at_norm, w_out, w_up, conv_w, conv_b, w_down, final_norm):
    n_batch, seq, d = x_prompt.shape
    n_seq = x_sample.shape[0]
    depth = w_in.shape[0]
    assert depth == 1 and x_sample.shape[1] == 1
    n_heads = state_hgrn.shape[2]
    assert cache_k.shape[3] == n_heads and cache_k.shape[4] == LANES and state_hgrn.shape[3] == LANES
    qk_dim = lam_q1.shape[1]
    past_len = page_table.shape[1] * cache_k.shape[2]
    lam_init = 0.8 - 0.6 * math.exp(-0.3 * 0)
    ff = w_down.shape[1]
    width = n_heads * LANES
    lam_params = [lam_q1, lam_k1, lam_q2, lam_k2]

    w_in16 = w_in[0].astype(BF16)
    w_out16 = w_out[0].astype(BF16)
    w_up16 = w_up[0].astype(BF16)
    w_down16 = w_down[0].astype(BF16)

    mod = _adaln(jnp.concatenate([c_prompt, c_sample], axis=0), w_ada[0], b_ada[0])
    mod_p = mod[:n_batch].reshape(n_batch, 1, N_MOD * d)
    mod_s = mod[n_batch:]

    tm_in = min(1024, seq)
    tps_in = seq // tm_in
    tm_out = min(512, seq)
    tps_out = seq // tm_out
    xp = x_prompt.reshape(n_batch * seq, d)
    tables_p = _rope_tables(jnp.arange(seq, dtype=jnp.int32), qk_dim)

    def pspec(part):
        return pl.BlockSpec((None, 1, d), lambda i, j: (i // tps_in, 0, part))

    zq, k_p, v_p = _inproj(xp, mod_p, mod_p, (pspec(0), pspec(1)), norm1, w_in16, tables_p,
                           pl.BlockSpec((tm_in, LANES), lambda i, j: (i % tps_in, 0)), tm_in, n_heads)

    xs = x_sample.reshape(n_seq, d)
    tables_s = _rope_tables(jnp.full((n_seq,), past_len, jnp.int32), qk_dim)

    def sspec(part):
        return pl.BlockSpec((n_seq, d), lambda i, j: (0, part))

    zq_s, k_s, v_s = _inproj(xs, mod_s, mod_s, (sspec(0), sspec(1)), norm1, w_in16, tables_s,
                             pl.BlockSpec((n_seq, LANES), lambda i, j: (0, 0)), n_seq, n_heads, split=1)
    zq_s4 = zq_s.reshape(5, n_seq, n_heads, LANES)

    y_at_s, y_hg, hgrn_p = _pages_and_hgrn(
        page_table, lam_params, at_norm, zq_s4[4], k_s.reshape(n_seq, n_heads, LANES),
        v_s.reshape(n_seq, n_heads, LANES), cache_k, cache_v, lam_init,
        zq, hg_lb_logits, hg_norm, n_batch, seq)

    y_at = _attn_prompt(lam_params, at_norm, zq, k_p, v_p, n_batch, seq, n_heads, lam_init)
    x1, h2 = _outproj(xp, y_hg, y_at, mod_p,
                      lambda part: pl.BlockSpec((None, 1, d), lambda i: (i // tps_out, 0, part)),
                      norm2, w_out16, tm_out)
    g_p, conv_p = _ffn_up(h2, w_up16, conv_w[0], conv_b, seq)
    y_p = _ffn_down(g_p, x1, mod_p, w_down16, final_norm.reshape(1, d), seq)

    y_hg_s, hgrn_s = _hgrn_sample(zq_s4, hg_lb_logits.reshape(-1, n_heads, LANES), hg_norm, state_hgrn[0])
    x1_s, h2_s = _outproj(xs, y_hg_s.reshape(n_seq, width), y_at_s.reshape(n_seq, width), mod_s,
                          lambda part: pl.BlockSpec((n_seq, d), lambda i: (0, part)),
                          norm2, w_out16, n_seq)
    prev = state_conv[0]
    y_s, a_s = _ffn_sample(x1_s, h2_s, mod_s, prev.reshape(n_seq, 2 * ff), w_up16, conv_w[0], conv_b,
                           w_down16, final_norm.reshape(1, d))
    conv_s = jnp.stack([prev[:, 1, :], a_s], axis=1)

    return (y_p.reshape(n_batch, seq, d), y_s.reshape(n_seq, 1, d),
            k_p.reshape(1, n_batch, seq, n_heads, LANES), v_p.reshape(1, n_batch, seq, n_heads, LANES),
            k_s.reshape(1, n_seq, 1, n_heads, LANES), v_s.reshape(1, n_seq, 1, n_heads, LANES),
            hgrn_p[None], hgrn_s[None], conv_p[None], conv_s[None])
```

```python
import functools
import math

import jax
import jax.numpy as jnp
from jax import lax
from jax.experimental import pallas as pl
from jax.experimental.pallas import tpu as pltpu

EPS = 1e-6
ROPE_THETA = 500000.0
LOG2_E = math.log2(math.e)
N_MOD = 6
LANES = 128
HGRN_CHUNK = 64
HGRN_SUB = 16
HGRN_TILE = 8
HALO = 16
VMEM_LIMIT_BYTES = 56 * 1024 * 1024

F32 = jnp.float32
BF16 = jnp.bfloat16


def _params(*sem):
    return pltpu.CompilerParams(dimension_semantics=sem, vmem_limit_bytes=VMEM_LIMIT_BYTES)


def _silu(x):
    return x / (1.0 + jnp.exp(-x))


def _sigmoid(x):
    return 1.0 / (1.0 + jnp.exp(-x))


def _rms(x):
    return x * lax.rsqrt(jnp.mean(x * x, axis=-1, keepdims=True) + EPS)


def _dot(a, b):
    return jnp.dot(a, b, preferred_element_type=F32)


def _dot_nt(a, b):
    return lax.dot_general(a, b, (((1,), (1,)), ((), ())), preferred_element_type=F32)


def _lam_value(lq1, lk1, lq2, lk2, lam_init):
    return (jnp.exp(jnp.sum(lq1 * lk1, axis=-1, keepdims=True))
            - jnp.exp(jnp.sum(lq2 * lk2, axis=-1, keepdims=True)) + lam_init)


def _lower_bound(layer_logits):
    mx = functools.reduce(jnp.maximum, layer_logits)
    es = [jnp.exp(r - mx) for r in layer_logits]
    return es[0] / functools.reduce(jnp.add, es)


def _adaln_kernel(c_ref, w_ref, b_ref, o_ref):
    s = _silu(c_ref[...]).astype(BF16)
    o_ref[...] = _dot(s, w_ref[...].astype(BF16)) + b_ref[...]


def _adaln(c_all, w_ada, b_ada, tn=512):
    rows, d = c_all.shape
    n = w_ada.shape[1]
    assert n % tn == 0
    return pl.pallas_call(
        _adaln_kernel,
        grid=(n // tn,),
        in_specs=[pl.BlockSpec((rows, d), lambda j: (0, 0)),
                  pl.BlockSpec((d, tn), lambda j: (0, j)),
                  pl.BlockSpec((1, tn), lambda j: (0, j))],
        out_specs=pl.BlockSpec((rows, tn), lambda j: (0, j)),
        out_shape=jax.ShapeDtypeStruct((rows, n), F32),
        compiler_params=_params("arbitrary"),
        name="adaln",
    )(c_all, w_ada, b_ada.reshape(1, n))


def _rope(blk, cos_t, sin_a, sin_b):
    return (blk * cos_t + pltpu.roll(blk, 8, axis=1) * sin_a
            + pltpu.roll(blk, LANES - 8, axis=1) * sin_b)


def _inproj_kernel(x_ref, sh_ref, sc_ref, g_ref, w_ref, cos_ref, sa_ref, sb_ref,
                   zq_ref, k_ref, v_ref, h_scr, *, split, sub_blocks):
    j = pl.program_id(1)
    part = j // split

    tm, tn = zq_ref.shape
    rows = tm // sub_blocks

    def rows_of(ref, rs):
        return ref[...] if ref.shape[0] == 1 else ref[rs, :]

    def project(dst, rotary=False, modulate=False):
        for r in range(sub_blocks):
            rs = slice(r * rows, (r + 1) * rows)
            if modulate:
                h = _rms(x_ref[rs, :]) * g_ref[...] * (1.0 + rows_of(sc_ref, rs)) + rows_of(sh_ref, rs)
                h_scr[rs, :] = h.astype(BF16)
            acc = _dot(h_scr[rs, :], w_ref[...])
            if rotary:
                for hh in range(tn // LANES):
                    sl = slice(hh * LANES, (hh + 1) * LANES)
                    dst[rs, sl] = _rope(acc[:, sl], cos_ref[rs, :], sa_ref[rs, :], sb_ref[rs, :])
            else:
                dst[rs, :] = acc

    @pl.when(j == 0)
    def _():
        project(zq_ref, modulate=True)

    @pl.when((j > 0) & (part < 4))
    def _():
        project(zq_ref)

    @pl.when(part == 4)
    def _():
        project(zq_ref, rotary=True)

    @pl.when(part == 5)
    def _():
        project(k_ref, rotary=True)

    @pl.when(part == 6)
    def _():
        project(v_ref)


def _inproj(x2, sh, sc, mod_spec, norm_g, w_in, tables, tab_spec, tm, n_heads, split=2):
    m, d = x2.shape
    width = n_heads * LANES
    assert w_in.shape[1] == 7 * width and n_heads % split == 0
    tn = width // split
    sh_spec, sc_spec = mod_spec

    def zq_map(i, j):
        jz = jnp.minimum(j, 5 * split - 1)
        return (jz // split, i, jz % split)

    return pl.pallas_call(
        functools.partial(_inproj_kernel, split=split, sub_blocks=max(1, tm // 256)),
        grid=(m // tm, 7 * split),
        in_specs=[pl.BlockSpec((tm, d), lambda i, j: (i, 0)),
                  sh_spec, sc_spec,
                  pl.BlockSpec((1, d), lambda i, j: (0, 0)),
                  pl.BlockSpec((d, tn), lambda i, j: (0, j)),
                  tab_spec, tab_spec, tab_spec],
        out_specs=[pl.BlockSpec((None, tm, tn), zq_map),
                   pl.BlockSpec((tm, tn), lambda i, j: (i, jnp.clip(j - 5 * split, 0, split - 1))),
                   pl.BlockSpec((tm, tn), lambda i, j: (i, jnp.clip(j - 6 * split, 0, split - 1)))],
        out_shape=[jax.ShapeDtypeStruct((5, m, width), F32),
                   jax.ShapeDtypeStruct((m, width), F32),
                   jax.ShapeDtypeStruct((m, width), F32)],
        scratch_shapes=[pltpu.VMEM((tm, d), BF16)],
        compiler_params=_params("arbitrary", "arbitrary"),
        name="inproj",
    )(x2, sh, sc, norm_g, w_in, *tables)


def _roll_in_tiles(x, shift):
    if shift == 0:
        return x
    tiles = [pltpu.roll(x[r:r + HGRN_TILE, :], shift, axis=0) for r in range(0, x.shape[0], HGRN_TILE)]
    return jnp.concatenate(tiles, axis=0)


def _hgrn_masks():
    c_rows = HGRN_CHUNK
    ri = lax.broadcasted_iota(jnp.int32, (c_rows, c_rows), 0)
    ci = lax.broadcasted_iota(jnp.int32, (c_rows, c_rows), 1)
    tri = jnp.where(ri >= ci, 1.0, 0.0).astype(BF16)
    same_sub = ri // HGRN_SUB == ci // HGRN_SUB
    same_tile = ri // HGRN_TILE == ci // HGRN_TILE
    diag = [same_tile & (ci == ri - dlt) for dlt in range(HGRN_TILE)]
    row = lax.broadcasted_iota(jnp.int32, (c_rows, 1), 0)
    return tri, same_sub, diag, row


def _hgrn_chunk(q, hf, v, g, lb, gn, st, masks):
    c_rows, sub, tile = HGRN_CHUNK, HGRN_SUB, HGRN_TILE
    tri, same_sub, diag, row = masks
    f = lb + (1.0 - lb) * _sigmoid(hf)
    kk = 1.0 - f
    lf = jnp.log(f)
    hi = lf.astype(BF16)
    r1 = lf - hi.astype(F32)
    mid = r1.astype(BF16)
    lo = (r1 - mid.astype(F32)).astype(BF16)
    b = _dot(tri, hi) + _dot(tri, mid) + _dot(tri, lo)
    b_last = b[c_rows - 1:c_rows, :]

    o = _dot_nt((q * jnp.exp(b)).astype(BF16), st.astype(BF16))
    kd = kk * jnp.exp(b_last - b)
    st_new = st * jnp.exp(b_last) + _dot(v.T.astype(BF16), kd.astype(BF16))

    v16 = v.astype(BF16)
    for jb in range(c_rows // sub - 1):
        e = (jb + 1) * sub
        r_ref = b[e - 1:e, :]
        qj = q * jnp.exp(jnp.minimum(b - r_ref, 0.0))
        kj = kk[jb * sub:e, :] * jnp.exp(r_ref - b[jb * sub:e, :])
        att = _dot_nt(qj.astype(BF16), kj.astype(BF16))
        att = jnp.where(row >= e, att, 0.0)
        o = o + _dot(att.astype(BF16), v16[jb * sub:e, :])
    zero_tile = jnp.zeros((tile, LANES), F32)
    q_parts, k_parts = [], []
    for r0 in range(0, c_rows, sub):
        r_ref = b[r0 + tile - 1:r0 + tile, :]
        first, second = slice(r0, r0 + tile), slice(r0 + tile, r0 + sub)
        k_parts += [kk[first, :] * jnp.exp(r_ref - b[first, :]), zero_tile]
        q_parts += [zero_tile, q[second, :] * jnp.exp(b[second, :] - r_ref)]
    att = _dot_nt(jnp.concatenate(q_parts, axis=0).astype(BF16), jnp.concatenate(k_parts, axis=0).astype(BF16))
    att = jnp.where(same_sub, att, 0.0)
    decay = None
    for dlt in range(tile):
        if dlt == 0:
            w = q * kk
        else:
            gate = _roll_in_tiles(f, dlt - 1)
            decay = gate if decay is None else decay * gate
            w = q * _roll_in_tiles(kk, dlt) * decay
        att = att + jnp.where(diag[dlt], jnp.sum(w, axis=-1, keepdims=True), 0.0)
    o = o + _dot(att.astype(BF16), v16)
    return _rms(o) * gn * _silu(g), st_new


def _hgrn_prompt_step(q_ref, f_ref, v_ref, g_ref, lbl_ref, gn_ref, y_ref, st_scr, group):
    c_rows = HGRN_CHUNK
    lb = _lower_bound([lbl_ref[l:l + 1, :] for l in range(lbl_ref.shape[0])])
    masks = _hgrn_masks()
    for r0 in range(0, q_ref.shape[0], c_rows):
        rs = slice(r0, r0 + c_rows)
        for gi in range(group):
            sl = slice(gi * LANES, (gi + 1) * LANES)
            y, st_new = _hgrn_chunk(q_ref[rs, sl], f_ref[rs, sl], v_ref[rs, sl], g_ref[rs, sl],
                                    lb[:, sl], gn_ref[...], st_scr[gi], masks)
            st_scr[gi] = st_new
            y_ref[rs, sl] = y


def _hgrn_sample_kernel(q_ref, f_ref, v_ref, g_ref, lbl_ref, gn_ref, s_ref, y_ref, so_ref,
                        *, n_heads):
    lb = _lower_bound([lbl_ref[l] for l in range(lbl_ref.shape[0])])
    pad = jnp.zeros((LANES - 3 * n_heads, LANES), F32)
    for s in range(q_ref.shape[0]):
        q = q_ref[s]
        v = v_ref[s]
        f = lb + (1.0 - lb) * _sigmoid(f_ref[s])
        kk = 1.0 - f
        cols = jnp.concatenate([f, kk, q, pad], axis=0).T
        o_rows = []
        for h in range(n_heads):
            f_col = cols[:, h:h + 1]
            k_col = cols[:, n_heads + h:n_heads + h + 1]
            q_col = cols[:, 2 * n_heads + h:2 * n_heads + h + 1]
            s_new = f_col * s_ref[s, h] + k_col * v[h:h + 1, :]
            so_ref[s, h] = s_new
            o_rows.append(jnp.sum(q_col * s_new, axis=0, keepdims=True))
        o = jnp.concatenate(o_rows, axis=0)
        y_ref[s] = _rms(o) * gn_ref[...] * _silu(g_ref[s])


def _hgrn_sample(zq4, lb_logits3, hg_norm, state, per_step=4):
    n_seq, n_heads = state.shape[0], state.shape[1]
    per_step = math.gcd(per_step, n_seq)

    def zspec(part):
        return pl.BlockSpec((None, per_step, n_heads, LANES), lambda b: (part, b, 0, 0))

    return pl.pallas_call(
        functools.partial(_hgrn_sample_kernel, n_heads=n_heads),
        grid=(n_seq // per_step,),
        in_specs=[zspec(0), zspec(1), zspec(2), zspec(3),
                  pl.BlockSpec(lb_logits3.shape, lambda b: (0, 0, 0)),
                  pl.BlockSpec((1, LANES), lambda b: (0, 0)),
                  pl.BlockSpec((per_step, n_heads, LANES, LANES), lambda b: (b, 0, 0, 0))],
        out_specs=[pl.BlockSpec((per_step, n_heads, LANES), lambda b: (b, 0, 0)),
                   pl.BlockSpec((per_step, n_heads, LANES, LANES), lambda b: (b, 0, 0, 0))],
        out_shape=[jax.ShapeDtypeStruct((n_seq, n_heads, LANES), F32),
                   jax.ShapeDtypeStruct(state.shape, F32)],
        compiler_params=_params("arbitrary"),
        name="hgrn_sample",
    )(zq4, zq4, zq4, zq4, lb_logits3, hg_norm, state)


def _attn_prompt_kernel(lq1_ref, lk1_ref, lq2_ref, lk2_ref, an_ref, q_ref, k_ref, v_ref, y_ref,
                        k16_scr, vt_scr, qm_scr, sa_scr, sb_scr, m_scr, l_scr, acc_scr, *, lam_init, tile,
                        qk_dim):
    qi = pl.program_id(2)
    n_kv = vt_scr.shape[0]
    tq = tk = tile

    @pl.when(qi == 0)
    def _():
        k16_scr[...] = k_ref[...].astype(BF16)
        for t in range(n_kv):
            vt_scr[t] = v_ref[t * tk:(t + 1) * tk, :].T.astype(BF16)

    q = q_ref[...] * (qk_dim ** -0.5 * LOG2_E)
    lane = lax.broadcasted_iota(jnp.int32, (1, LANES), 1)
    qm_scr[:tq, :] = jnp.where(lane < qk_dim, q, 0.0).astype(BF16)
    qm_scr[tq:, :] = jnp.where(lane >= qk_dim, q, 0.0).astype(BF16)
    m_scr[...] = jnp.full_like(m_scr, -jnp.inf)
    l_scr[...] = jnp.zeros_like(l_scr)
    acc_scr[...] = jnp.zeros_like(acc_scr)

    def scores(kj, dst):
        r0 = pl.multiple_of(kj * tk, tk)
        dst[...] = _dot_nt(k16_scr[pl.ds(r0, tk), :], qm_scr[...])

    def consume(kj, src, on_diagonal):
        st = src[...]
        if on_diagonal:
            key = lax.broadcasted_iota(jnp.int32, (tk, 2 * tq), 0)
            qry = lax.broadcasted_iota(jnp.int32, (tk, 2 * tq), 1) % tq
            st = jnp.where(key <= qry, st, -jnp.inf)
        m_old = m_scr[...]
        m_new = jnp.maximum(m_old, jnp.max(st, axis=0, keepdims=True))
        p = jnp.exp2(st - m_new)
        corr = jnp.exp2(m_old - m_new)
        l_scr[...] = l_scr[...] * corr + jnp.sum(p.reshape(tk // 8, 8, 2 * tq), axis=0)
        acc_scr[...] = acc_scr[...] * corr + _dot(vt_scr[kj], p.astype(BF16))
        m_scr[...] = m_new

    scores(0, sa_scr)

    def body(pair, carry):
        kj = 2 * pair
        scores(kj + 1, sb_scr)
        consume(kj, sa_scr, False)
        scores(kj + 2, sa_scr)
        consume(kj + 1, sb_scr, False)
        return carry

    lax.fori_loop(0, qi // 2, body, 0)

    @pl.when(qi % 2 == 0)
    def _():
        consume(qi, sa_scr, True)

    @pl.when(qi % 2 == 1)
    def _():
        scores(qi, sb_scr)
        consume(qi - 1, sa_scr, False)
        consume(qi, sb_scr, True)

    lam = _lam_value(lq1_ref[...], lk1_ref[...], lq2_ref[...], lk2_ref[...], lam_init)
    o = acc_scr[...] / jnp.sum(l_scr[...], axis=0, keepdims=True)
    o = (o[:, :tq] - lam * o[:, tq:]).T
    y_ref[...] = _rms(o) * an_ref[...] * (1.0 - lam_init)


def _attn_prompt(lam_params, at_norm, zq, k, v, n_batch, seq, n_heads, lam_init, tile=512):
    m = k.shape[0]
    nt = seq // tile
    qk_dim = lam_params[0].shape[1]
    small = [pl.BlockSpec(p.shape, lambda b, h, i: (0, 0)) for p in lam_params]
    kv_spec = pl.BlockSpec((seq, LANES), lambda b, h, i: (b, h))

    return pl.pallas_call(
        functools.partial(_attn_prompt_kernel, lam_init=lam_init, tile=tile, qk_dim=qk_dim),
        grid=(n_batch, n_heads, nt),
        in_specs=small + [pl.BlockSpec((1, LANES), lambda b, h, i: (0, 0)),
                          pl.BlockSpec((None, tile, LANES), lambda b, h, i: (4, b * nt + i, h)),
                          kv_spec, kv_spec],
        out_specs=pl.BlockSpec((tile, LANES), lambda b, h, i: (b * nt + i, h)),
        out_shape=jax.ShapeDtypeStruct((m, n_heads * LANES), F32),
        scratch_shapes=[pltpu.VMEM((seq, LANES), BF16), pltpu.VMEM((nt, LANES, tile), BF16),
                        pltpu.VMEM((2 * tile, LANES), BF16),
                        pltpu.VMEM((tile, 2 * tile), F32), pltpu.VMEM((tile, 2 * tile), F32),
                        pltpu.VMEM((1, 2 * tile), F32),
                        pltpu.VMEM((8, 2 * tile), F32), pltpu.VMEM((LANES, 2 * tile), F32)],
        compiler_params=_params("arbitrary", "arbitrary", "arbitrary"),
        name="attn_prompt",
    )(*lam_params, at_norm, zq, k, v)


def _pages_hgrn_kernel(pt_ref, lq1_ref, lk1_ref, lq2_ref, lk2_ref, an_ref, q_ref, kn_ref, vn_ref,
                       z_ref, lbl_ref, gn_ref, *rest,
                       lam_init, n_heads, qk_dim, pages, group, sweep_steps):
    k_refs = rest[:pages]
    v_refs = rest[pages:2 * pages]
    y_ref, yh_ref, sfin_ref, m_scr, l_scr, acc_scr, st_scr = rest[2 * pages:]
    j = pl.program_id(1)
    page_rows = k_refs[0].shape[0] * n_heads
    sweep_pos = (pl.program_id(0) * pl.num_programs(1) + j) % sweep_steps

    @pl.when(sweep_pos == 0)
    def _():
        st_scr[...] = jnp.zeros_like(st_scr)

    q = q_ref[...] * (qk_dim ** -0.5)
    lane = lax.broadcasted_iota(jnp.int32, (1, LANES), 1)
    qm = jnp.concatenate([jnp.where(lane < qk_dim, q, 0.0), jnp.where(lane >= qk_dim, q, 0.0)], axis=0)

    @pl.when(j == 0)
    def _():
        kn = kn_ref[...]
        vn = vn_ref[...]
        m_scr[...] = jnp.sum(qm * jnp.concatenate([kn, kn], axis=0), axis=-1, keepdims=True)
        l_scr[...] = jnp.ones_like(l_scr)
        acc_scr[...] = jnp.concatenate([vn, vn], axis=0)

    qm16 = qm.astype(BF16)
    own_head = (lax.broadcasted_iota(jnp.int32, (2 * n_heads, page_rows), 1) % n_heads
                == lax.broadcasted_iota(jnp.int32, (2 * n_heads, page_rows), 0) % n_heads)
    _hgrn_prompt_step(z_ref.at[0], z_ref.at[1], z_ref.at[2], z_ref.at[3], lbl_ref, gn_ref, yh_ref, st_scr,
                      group)

    scores = [jnp.where(own_head, _dot_nt(qm16, k_refs[p][...].reshape(page_rows, LANES).astype(BF16)),
                        -jnp.inf) for p in range(pages)]
    m_old = m_scr[...]
    m_new = functools.reduce(jnp.maximum, [jnp.max(s, axis=-1, keepdims=True) for s in scores] + [m_old])
    corr = jnp.exp(m_old - m_new)
    l_new = l_scr[...] * corr
    acc = acc_scr[...] * corr
    for p in range(pages):
        pr = jnp.exp(scores[p] - m_new)
        l_new = l_new + jnp.sum(pr, axis=-1, keepdims=True)
        acc = acc + _dot(pr.astype(BF16), v_refs[p][...].reshape(page_rows, LANES).astype(BF16))
    l_scr[...] = l_new
    acc_scr[...] = acc
    m_scr[...] = m_new

    @pl.when(sweep_pos == sweep_steps - 1)
    def _():
        for gi in range(group):
            sfin_ref[gi] = st_scr[gi].T

    @pl.when(j == pl.num_programs(1) - 1)
    def _():
        lam = _lam_value(lq1_ref[...], lk1_ref[...], lq2_ref[...], lk2_ref[...], lam_init)
        o = acc_scr[...] / l_scr[...]
        w = o[:n_heads] - lam * o[n_heads:]
        y_ref[...] = _rms(w) * an_ref[...] * (1.0 - lam_init)


def _pages_and_hgrn(page_table, lam_params, at_norm, q3, kn3, vn3, cache_k, cache_v, lam_init,
                    zq, lb_logits, hg_norm, n_batch, seq, pages=16, group=8):
    n_seq, n_pages = page_table.shape
    _, _, page_size, n_heads, _ = cache_k.shape
    qk_dim = lam_params[0].shape[1]
    assert n_pages % pages == 0
    page_steps = n_pages // pages
    n_steps = n_seq * page_steps
    group = math.gcd(group, n_heads)
    n_groups = n_heads // group
    gw = group * LANES
    m = zq.shape[1]
    assert (n_batch * n_groups * seq) % (n_steps * HGRN_CHUNK) == 0
    rows = n_batch * n_groups * seq // n_steps
    assert seq % rows == 0
    sweep_steps = seq // rows

    small = [pl.BlockSpec(p.shape, lambda b, j, pt: (0, 0)) for p in lam_params]
    row_spec = pl.BlockSpec((None, n_heads, LANES), lambda b, j, pt: (b, 0, 0))

    def page_spec(p):
        return pl.BlockSpec(
            (None, None, page_size, n_heads, LANES),
            lambda b, j, pt: (0, pt[b * n_pages + j * pages + p], 0, 0, 0))

    def sweep(b, j):
        step = b * page_steps + j
        return step // (sweep_steps * n_groups), (step // sweep_steps) % n_groups, step % sweep_steps

    def rows_map(b, j, pt):
        sb, sg, sp = sweep(b, j)
        return (sb * sweep_steps + sp, sg)

    z_spec = pl.BlockSpec((4, rows, gw), lambda b, j, pt: (0,) + rows_map(b, j, pt))

    grid_spec = pltpu.PrefetchScalarGridSpec(
        num_scalar_prefetch=1,
        grid=(n_seq, page_steps),
        in_specs=(small + [pl.BlockSpec((1, LANES), lambda b, j, pt: (0, 0)), row_spec, row_spec, row_spec]
                  + [z_spec,
                     pl.BlockSpec((lb_logits.shape[0], gw), lambda b, j, pt: (0, sweep(b, j)[1])),
                     pl.BlockSpec((1, LANES), lambda b, j, pt: (0, 0))]
                  + [page_spec(p) for p in range(pages)] + [page_spec(p) for p in range(pages)]),
        out_specs=[row_spec,
                   pl.BlockSpec((rows, gw), rows_map),
                   pl.BlockSpec((None, group, LANES, LANES),
                                lambda b, j, pt: (sweep(b, j)[0], sweep(b, j)[1], 0, 0))],
        scratch_shapes=[pltpu.VMEM((2 * n_heads, 1), F32), pltpu.VMEM((2 * n_heads, 1), F32),
                        pltpu.VMEM((2 * n_heads, LANES), F32), pltpu.VMEM((group, LANES, LANES), F32)],
    )
    return pl.pallas_call(
        functools.partial(_pages_hgrn_kernel, lam_init=lam_init, n_heads=n_heads, qk_dim=qk_dim,
                          pages=pages, group=group, sweep_steps=sweep_steps),
        grid_spec=grid_spec,
        out_shape=[jax.ShapeDtypeStruct((n_seq, n_heads, LANES), F32),
                   jax.ShapeDtypeStruct((m, n_heads * LANES), F32),
                   jax.ShapeDtypeStruct((n_batch, n_heads, LANES, LANES), F32)],
        compiler_params=_params("arbitrary", "arbitrary"),
        name="pages_hgrn",
    )(page_table.reshape(-1), *lam_params, at_norm, q3, kn3, vn3, zq, lb_logits, hg_norm,
      *([cache_k] * pages), *([cache_v] * pages))


def _outproj_kernel(x_ref, yh_ref, ya_ref, g1_ref, sh_ref, sc_ref, n2_ref, w_ref, o_ref, h_ref):
    hw = yh_ref.shape[1]
    rows = min(256, x_ref.shape[0])
    for r in range(x_ref.shape[0] // rows):
        rs = slice(r * rows, (r + 1) * rows)
        acc = (_dot(yh_ref[rs, :].astype(BF16), w_ref[:hw, :])
               + _dot(ya_ref[rs, :].astype(BF16), w_ref[hw:, :]))
        x1 = x_ref[rs, :] + g1_ref[...] * acc
        o_ref[rs, :] = x1
        h_ref[rs, :] = (_rms(x1) * n2_ref[...] * (1.0 + sc_ref[...]) + sh_ref[...]).astype(BF16)


def _outproj(x2, y_hg, y_at, mod, mod_spec, norm2, w_out, tm):
    m, d = x2.shape
    return pl.pallas_call(
        _outproj_kernel,
        grid=(m // tm,),
        in_specs=[pl.BlockSpec((tm, d), lambda i: (i, 0)),
                  pl.BlockSpec((tm, y_hg.shape[1]), lambda i: (i, 0)),
                  pl.BlockSpec((tm, y_at.shape[1]), lambda i: (i, 0)),
                  mod_spec(2), mod_spec(3), mod_spec(4),
                  pl.BlockSpec((1, d), lambda i: (0, 0)),
                  pl.BlockSpec(w_out.shape, lambda i: (0, 0))],
        out_specs=[pl.BlockSpec((tm, d), lambda i: (i, 0)),
                   pl.BlockSpec((tm, d), lambda i: (i, 0))],
        out_shape=[jax.ShapeDtypeStruct((m, d), F32),
                   jax.ShapeDtypeStruct((m, d), BF16)],
        compiler_params=_params("arbitrary"),
        name="outproj",
    )(x2, y_hg, y_at, mod, mod, mod, norm2, w_out)


def _conv_gate(conv, u):
    return (_silu(conv) * u).astype(BF16)


def _ffn_up_kernel(h_ref, halo_ref, wa_ref, wu_ref, cw_ref, cb_ref, g_ref, tail_ref, h_scr, a_scr,
                   *, tm, tiles_per_seq, sub_blocks):
    i = pl.program_id(0)

    @pl.when(pl.program_id(1) == 0)
    def _():
        h_scr[:HALO, :] = halo_ref[...]
        h_scr[HALO:, :] = h_ref[...]

    rows = tm // sub_blocks
    for r in range(sub_blocks):
        lo = HALO + r * rows
        if r == 0:
            a_scr[:lo + rows, :] = _dot(h_scr[:lo + rows, :], wa_ref[...])

            @pl.when(i % tiles_per_seq == 0)
            def _():
                a_scr[:HALO, :] = jnp.zeros((HALO, a_scr.shape[1]), F32)
        else:
            a_scr[lo:lo + rows, :] = _dot(h_scr[lo:lo + rows, :], wa_ref[...])
        u = _dot(h_scr[lo:lo + rows, :], wu_ref[...])
        conv = (cb_ref[...] + a_scr[lo - 2:lo - 2 + rows, :] * cw_ref[0:1, :]
                + a_scr[lo - 1:lo - 1 + rows, :] * cw_ref[1:2, :]
                + a_scr[lo:lo + rows, :] * cw_ref[2:3, :])
        g_ref[r * rows:(r + 1) * rows, :] = _conv_gate(conv, u)
    tail_ref[...] = a_scr[HALO + tm - 2:, :]


def _ffn_up(h2, w_up, conv_w, conv_b, seq, tm=1024, tf=512, sub_blocks=4):
    m, d = h2.shape
    ff = w_up.shape[1] // 2
    nf = ff // tf
    tm = min(tm, seq)
    tps = seq // tm
    assert conv_w.shape[0] == 3 and ff % tf == 0 and seq % tm == 0
    g, tails = pl.pallas_call(
        functools.partial(_ffn_up_kernel, tm=tm, tiles_per_seq=tps, sub_blocks=sub_blocks),
        grid=(m // tm, nf),
        in_specs=[pl.BlockSpec((tm, d), lambda i, f: (i, 0)),
                  pl.BlockSpec((HALO, d), lambda i, f: (jnp.maximum(i * (tm // HALO) - 1, 0), 0)),
                  pl.BlockSpec((d, tf), lambda i, f: (0, f)),
                  pl.BlockSpec((d, tf), lambda i, f: (0, nf + f)),
                  pl.BlockSpec((3, tf), lambda i, f: (0, f)),
                  pl.BlockSpec((1, tf), lambda i, f: (0, f))],
        out_specs=[pl.BlockSpec((tm, tf), lambda i, f: (i, f)),
                   pl.BlockSpec((None, 2, tf), lambda i, f: (i, 0, f))],
        out_shape=[jax.ShapeDtypeStruct((m, ff), BF16),
                   jax.ShapeDtypeStruct((m // tm, 2, ff), F32)],
        scratch_shapes=[pltpu.VMEM((HALO + tm, d), BF16), pltpu.VMEM((HALO + tm, tf), F32)],
        compiler_params=_params("arbitrary", "arbitrary"),
        name="ffn_up",
    )(h2, h2, w_up, w_up, conv_w, conv_b)
    return g, tails[tps - 1::tps]


def _ffn_down_kernel(g_ref, x_ref, g2_ref, wd_ref, fn_ref, y_ref, ss_scr, *, tn, sub_blocks):
    n = pl.program_id(1)
    d = y_ref.shape[1]

    @pl.when(n == 0)
    def _():
        ss_scr[...] = jnp.zeros_like(ss_scr)

    rows = y_ref.shape[0] // sub_blocks
    for r in range(sub_blocks):
        rs = slice(r * rows, (r + 1) * rows)
        x2 = x_ref[rs, :] + g2_ref[...] * _dot(g_ref[rs, :], wd_ref[...])
        ss_scr[rs, :] += jnp.sum(x2 * x2, axis=-1, keepdims=True)
        y_ref[rs, pl.ds(pl.multiple_of(n * tn, tn), tn)] = x2

    @pl.when(n == d // tn - 1)
    def _():
        y_ref[...] = y_ref[...] * lax.rsqrt(ss_scr[...] / d + EPS) * fn_ref[...]


def _ffn_down(g, x1, mod_p, w_down, final_norm, seq, tm=1024, tn=256, sub_blocks=4):
    m, d = x1.shape
    ff = w_down.shape[0]
    tm = min(tm, seq)
    tps = seq // tm
    nd = d // tn
    assert d % tn == 0 and seq % tm == 0
    return pl.pallas_call(
        functools.partial(_ffn_down_kernel, tn=tn, sub_blocks=sub_blocks),
        grid=(m // tm, nd),
        in_specs=[pl.BlockSpec((tm, ff), lambda i, n: (i, 0)),
                  pl.BlockSpec((tm, tn), lambda i, n: (i, n)),
                  pl.BlockSpec((None, 1, tn), lambda i, n: (i // tps, 0, 5 * nd + n)),
                  pl.BlockSpec((ff, tn), lambda i, n: (0, n)),
                  pl.BlockSpec((1, d), lambda i, n: (0, 0))],
        out_specs=pl.BlockSpec((tm, d), lambda i, n: (i, 0)),
        out_shape=jax.ShapeDtypeStruct((m, d), F32),
        scratch_shapes=[pltpu.VMEM((tm, 1), F32)],
        compiler_params=_params("arbitrary", "arbitrary"),
        name="ffn_down",
    )(g, x1, mod_p, w_down, final_norm)


def _ffn_sample_kernel(x_ref, h_ref, g2_ref, p0_ref, p1_ref, wa_ref, wu_ref, cw_ref,
                       cb_ref, wd_ref, fn_ref, y_ref, a_ref, acc_scr):
    f = pl.program_id(0)

    @pl.when(f == 0)
    def _():
        acc_scr[...] = jnp.zeros_like(acc_scr)

    a = _dot(h_ref[...], wa_ref[...])
    u = _dot(h_ref[...], wu_ref[...])
    conv = (cb_ref[...] + p0_ref[...] * cw_ref[0:1, :] + p1_ref[...] * cw_ref[1:2, :]
            + a * cw_ref[2:3, :])
    acc_scr[...] += _dot(_conv_gate(conv, u), wd_ref[...])
    a_ref[...] = a

    @pl.when(f == pl.num_programs(0) - 1)
    def _():
        y_ref[...] = _rms(x_ref[...] + g2_ref[...] * acc_scr[...]) * fn_ref[...]


def _ffn_sample(x1, h2, mod_s, prev2, w_up, conv_w, conv_b, w_down, final_norm, tf=512):
    m, d = x1.shape
    ff = w_down.shape[0]
    nf = ff // tf

    return pl.pallas_call(
        _ffn_sample_kernel,
        grid=(nf,),
        in_specs=[pl.BlockSpec((m, d), lambda f: (0, 0)),
                  pl.BlockSpec((m, d), lambda f: (0, 0)),
                  pl.BlockSpec((m, d), lambda f: (0, 5)),
                  pl.BlockSpec((m, tf), lambda f: (0, f)),
                  pl.BlockSpec((m, tf), lambda f: (0, nf + f)),
                  pl.BlockSpec((d, tf), lambda f: (0, f)),
                  pl.BlockSpec((d, tf), lambda f: (0, nf + f)),
                  pl.BlockSpec((3, tf), lambda f: (0, f)),
                  pl.BlockSpec((1, tf), lambda f: (0, f)),
                  pl.BlockSpec((tf, d), lambda f: (f, 0)),
                  pl.BlockSpec((1, d), lambda f: (0, 0))],
        out_specs=[pl.BlockSpec((m, d), lambda f: (0, 0)),
                   pl.BlockSpec((m, tf), lambda f: (0, f))],
        out_shape=[jax.ShapeDtypeStruct((m, d), F32),
                   jax.ShapeDtypeStruct((m, ff), F32)],
        scratch_shapes=[pltpu.VMEM((m, d), F32)],
        compiler_params=_params("arbitrary"),
        name="ffn_sample",
    )(x1, h2, mod_s, prev2, prev2, w_up, w_up, conv_w, conv_b, w_down, final_norm)


def _rope_tables(pos, qk_dim):
    rot = qk_dim // 4
    half = rot // 2
    inv_freq = ROPE_THETA ** (-jnp.arange(half, dtype=F32) / half)
    ang = pos.astype(F32)[:, None] * inv_freq[None, :]
    cos, sin = jnp.cos(ang), jnp.sin(ang)
    ones = jnp.ones((pos.shape[0], qk_dim - rot), F32)
    zeros = jnp.zeros((pos.shape[0], qk_dim - rot), F32)
    zh = jnp.zeros_like(sin)
    cos_t = jnp.concatenate([cos, cos, ones] * 2, axis=1)
    sin_a = jnp.concatenate([zh, sin, zeros] * 2, axis=1)
    sin_b = jnp.concatenate([-sin, zh, zeros] * 2, axis=1)
    return cos_t, sin_a, sin_b


def kernel(x_prompt, x_sample, c_prompt, c_sample, cache_k, cache_v, state_hgrn, state_conv, page_table,
           w_ada, b_ada, norm1, norm2, w_in, hg_lb_logits, hg_norm, lam_q1, lam_k1, lam_q2, lam_k2,
           at_norm, w_out, w_up, conv_w, conv_b, w_down, final_norm):
    n_batch, seq, d = x_prompt.shape
    n_seq = x_sample.shape[0]
    depth = w_in.shape[0]
    assert depth == 1 and x_sample.shape[1] == 1
    n_heads = state_hgrn.shape[2]
    assert cache_k.shape[3] == n_heads and cache_k.shape[4] == LANES and state_hgrn.shape[3] == LANES
    qk_dim = lam_q1.shape[1]
    past_len = page_table.shape[1] * cache_k.shape[2]
    lam_init = 0.8 - 0.6 * math.exp(-0.3 * 0)
    ff = w_down.shape[1]
    width = n_heads * LANES
    lam_params = [lam_q1, lam_k1, lam_q2, lam_k2]

    w_in16 = w_in[0].astype(BF16)
    w_out16 = w_out[0].astype(BF16)
    w_up16 = w_up[0].astype(BF16)
    w_down16 = w_down[0].astype(BF16)

    mod = _adaln(jnp.concatenate([c_prompt, c_sample], axis=0), w_ada[0], b_ada[0])
    mod_p = mod[:n_batch].reshape(n_batch, 1, N_MOD * d)
    mod_s = mod[n_batch:]

    tm_in = min(1024, seq)
    tps_in = seq // tm_in
    tm_out = min(512, seq)
    tps_out = seq // tm_out
    xp = x_prompt.reshape(n_batch * seq, d)
    tables_p = _rope_tables(jnp.arange(seq, dtype=jnp.int32), qk_dim)

    def pspec(part):
        return pl.BlockSpec((None, 1, d), lambda i, j: (i // tps_in, 0, part))

    zq, k_p, v_p = _inproj(xp, mod_p, mod_p, (pspec(0), pspec(1)), norm1, w_in16, tables_p,
                           pl.BlockSpec((tm_in, LANES), lambda i, j: (i % tps_in, 0)), tm_in, n_heads)

    xs = x_sample.reshape(n_seq, d)
    tables_s = _rope_tables(jnp.full((n_seq,), past_len, jnp.int32), qk_dim)

    def sspec(part):
        return pl.BlockSpec((n_seq, d), lambda i, j: (0, part))

    zq_s, k_s, v_s = _inproj(xs, mod_s, mod_s, (sspec(0), sspec(1)), norm1, w_in16, tables_s,
                             pl.BlockSpec((n_seq, LANES), lambda i, j: (0, 0)), n_seq, n_heads, split=1)
    zq_s4 = zq_s.reshape(5, n_seq, n_heads, LANES)

    y_at_s, y_hg, hgrn_p = _pages_and_hgrn(
        page_table, lam_params, at_norm, zq_s4[4], k_s.reshape(n_seq, n_heads, LANES),
        v_s.reshape(n_seq, n_heads, LANES), cache_k, cache_v, lam_init,
        zq, hg_lb_logits, hg_norm, n_batch, seq)

    y_at = _attn_prompt(lam_params, at_norm, zq, k_p, v_p, n_batch, seq, n_heads, lam_init)
    x1, h2 = _outproj(xp, y_hg, y_at, mod_p,
                      lambda part: pl.BlockSpec((None, 1, d), lambda i: (i // tps_out, 0, part)),
                      norm2, w_out16, tm_out)
    g_p, conv_p = _ffn_up(h2, w_up16, conv_w[0], conv_b, seq)
    y_p = _ffn_down(g_p, x1, mod_p, w_down16, final_norm.reshape(1, d), seq)

    y_hg_s, hgrn_s = _hgrn_sample(zq_s4, hg_lb_logits.reshape(-1, n_heads, LANES), hg_norm, state_hgrn[0])
    x1_s, h2_s = _outproj(xs, y_hg_s.reshape(n_seq, width), y_at_s.reshape(n_seq, width), mod_s,
                          lambda part: pl.BlockSpec((n_seq, d), lambda i: (0, part)),
                          norm2, w_out16, n_seq)
    prev = state_conv[0]
    y_s, a_s = _ffn_sample(x1_s, h2_s, mod_s, prev.reshape(n_seq, 2 * ff), w_up16, conv_w[0], conv_b,
                           w_down16, final_norm.reshape(1, d))
    conv_s = jnp.stack([prev[:, 1, :], a_s], axis=1)

    return (y_p.reshape(n_batch, seq, d), y_s.reshape(n_seq, 1, d),
            k_p.reshape(1, n_batch, seq, n_heads, LANES), v_p.reshape(1, n_batch, seq, n_heads, LANES),
            k_s.reshape(1, n_seq, 1, n_heads, LANES), v_s.reshape(1, n_seq, 1, n_heads, LANES),
            hgrn_p[None], hgrn_s[None], conv_p[None], conv_s[None])
```

```python
import functools
import math

import jax
import jax.numpy as jnp
from jax import lax
from jax.experimental import pallas as pl
from jax.experimental.pallas import tpu as pltpu

EPS = 1e-6
ROPE_THETA = 500000.0
LOG2_E = math.log2(math.e)
N_MOD = 6
LANES = 128
HGRN_CHUNK = 64
HGRN_SUB = 16
HGRN_TILE = 8
HALO = 16
VMEM_LIMIT_BYTES = 56 * 1024 * 1024

F32 = jnp.float32
BF16 = jnp.bfloat16


def _params(*sem):
    return pltpu.CompilerParams(dimension_semantics=sem, vmem_limit_bytes=VMEM_LIMIT_BYTES)


def _silu(x):
    return x / (1.0 + jnp.exp(-x))


def _sigmoid(x):
    return 1.0 / (1.0 + jnp.exp(-x))


def _rms(x):
    return x * lax.rsqrt(jnp.mean(x * x, axis=-1, keepdims=True) + EPS)


def _dot(a, b):
    return jnp.dot(a, b, preferred_element_type=F32)


def _dot_nt(a, b):
    return lax.dot_general(a, b, (((1,), (1,)), ((), ())), preferred_element_type=F32)


def _lam_value(lq1, lk1, lq2, lk2, lam_init):
    return (jnp.exp(jnp.sum(lq1 * lk1, axis=-1, keepdims=True))
            - jnp.exp(jnp.sum(lq2 * lk2, axis=-1, keepdims=True)) + lam_init)


def _lower_bound(layer_logits):
    mx = functools.reduce(jnp.maximum, layer_logits)
    es = [jnp.exp(r - mx) for r in layer_logits]
    return es[0] / functools.reduce(jnp.add, es)


def _adaln_kernel(c_ref, w_ref, b_ref, o_ref):
    s = _silu(c_ref[...]).astype(BF16)
    o_ref[...] = _dot(s, w_ref[...].astype(BF16)) + b_ref[...]


def _adaln(c_all, w_ada, b_ada, tn=512):
    rows, d = c_all.shape
    n = w_ada.shape[1]
    assert n % tn == 0
    return pl.pallas_call(
        _adaln_kernel,
        grid=(n // tn,),
        in_specs=[pl.BlockSpec((rows, d), lambda j: (0, 0)),
                  pl.BlockSpec((d, tn), lambda j: (0, j)),
                  pl.BlockSpec((1, tn), lambda j: (0, j))],
        out_specs=pl.BlockSpec((rows, tn), lambda j: (0, j)),
        out_shape=jax.ShapeDtypeStruct((rows, n), F32),
        compiler_params=_params("arbitrary"),
        name="adaln",
    )(c_all, w_ada, b_ada.reshape(1, n))


def _rope(blk, cos_t, sin_a, sin_b):
    return (blk * cos_t + pltpu.roll(blk, 8, axis=1) * sin_a
            + pltpu.roll(blk, LANES - 8, axis=1) * sin_b)


def _inproj_kernel(x_ref, sh_ref, sc_ref, g_ref, w_ref, cos_ref, sa_ref, sb_ref,
                   zq_ref, k_ref, v_ref, h_scr, *, split, sub_blocks):
    j = pl.program_id(1)
    part = j // split

    tm, tn = zq_ref.shape
    rows = tm // sub_blocks

    def rows_of(ref, rs):
        return ref[...] if ref.shape[0] == 1 else ref[rs, :]

    def project(dst, rotary=False, modulate=False):
        for r in range(sub_blocks):
            rs = slice(r * rows, (r + 1) * rows)
            if modulate:
                h = _rms(x_ref[rs, :]) * g_ref[...] * (1.0 + rows_of(sc_ref, rs)) + rows_of(sh_ref, rs)
                h_scr[rs, :] = h.astype(BF16)
            acc = _dot(h_scr[rs, :], w_ref[...])
            if rotary:
                for hh in range(tn // LANES):
                    sl = slice(hh * LANES, (hh + 1) * LANES)
                    dst[rs, sl] = _rope(acc[:, sl], cos_ref[rs, :], sa_ref[rs, :], sb_ref[rs, :])
            else:
                dst[rs, :] = acc

    @pl.when(j == 0)
    def _():
        project(zq_ref, modulate=True)

    @pl.when((j > 0) & (part < 4))
    def _():
        project(zq_ref)

    @pl.when(part == 4)
    def _():
        project(zq_ref, rotary=True)

    @pl.when(part == 5)
    def _():
        project(k_ref, rotary=True)

    @pl.when(part == 6)
    def _():
        project(v_ref)


def _inproj(x2, sh, sc, mod_spec, norm_g, w_in, tables, tab_spec, tm, n_heads, split=2):
    m, d = x2.shape
    width = n_heads * LANES
    assert w_in.shape[1] == 7 * width and n_heads % split == 0
    tn = width // split
    sh_spec, sc_spec = mod_spec

    def zq_map(i, j):
        jz = jnp.minimum(j, 5 * split - 1)
        return (jz // split, i, jz % split)

    return pl.pallas_call(
        functools.partial(_inproj_kernel, split=split, sub_blocks=max(1, tm // 256)),
        grid=(m // tm, 7 * split),
        in_specs=[pl.BlockSpec((tm, d), lambda i, j: (i, 0)),
                  sh_spec, sc_spec,
                  pl.BlockSpec((1, d), lambda i, j: (0, 0)),
                  pl.BlockSpec((d, tn), lambda i, j: (0, j)),
                  tab_spec, tab_spec, tab_spec],
        out_specs=[pl.BlockSpec((None, tm, tn), zq_map),
                   pl.BlockSpec((tm, tn), lambda i, j: (i, jnp.clip(j - 5 * split, 0, split - 1))),
                   pl.BlockSpec((tm, tn), lambda i, j: (i, jnp.clip(j - 6 * split, 0, split - 1)))],
        out_shape=[jax.ShapeDtypeStruct((5, m, width), F32),
                   jax.ShapeDtypeStruct((m, width), F32),
                   jax.ShapeDtypeStruct((m, width), F32)],
        scratch_shapes=[pltpu.VMEM((tm, d), BF16)],
        compiler_params=_params("arbitrary", "arbitrary"),
        name="inproj",
    )(x2, sh, sc, norm_g, w_in, *tables)


def _roll_in_tiles(x, shift):
    if shift == 0:
        return x
    tiles = [pltpu.roll(x[r:r + HGRN_TILE, :], shift, axis=0) for r in range(0, x.shape[0], HGRN_TILE)]
    return jnp.concatenate(tiles, axis=0)


def _hgrn_masks():
    c_rows = HGRN_CHUNK
    ri = lax.broadcasted_iota(jnp.int32, (c_rows, c_rows), 0)
    ci = lax.broadcasted_iota(jnp.int32, (c_rows, c_rows), 1)
    tri = jnp.where(ri >= ci, 1.0, 0.0).astype(BF16)
    same_sub = ri // HGRN_SUB == ci // HGRN_SUB
    same_tile = ri // HGRN_TILE == ci // HGRN_TILE
    diag = [same_tile & (ci == ri - dlt) for dlt in range(HGRN_TILE)]
    row = lax.broadcasted_iota(jnp.int32, (c_rows, 1), 0)
    return tri, same_sub, diag, row


def _hgrn_chunk(q, hf, v, g, lb, gn, st, masks):
    c_rows, sub, tile = HGRN_CHUNK, HGRN_SUB, HGRN_TILE
    tri, same_sub, diag, row = masks
    f = lb + (1.0 - lb) * _sigmoid(hf)
    kk = 1.0 - f
    lf = jnp.log(f)
    hi = lf.astype(BF16)
    r1 = lf - hi.astype(F32)
    mid = r1.astype(BF16)
    lo = (r1 - mid.astype(F32)).astype(BF16)
    b = _dot(tri, hi) + _dot(tri, mid) + _dot(tri, lo)
    b_last = b[c_rows - 1:c_rows, :]

    o = _dot_nt((q * jnp.exp(b)).astype(BF16), st.astype(BF16))
    kd = kk * jnp.exp(b_last - b)
    st_new = st * jnp.exp(b_last) + _dot(v.T.astype(BF16), kd.astype(BF16))

    v16 = v.astype(BF16)
    for jb in range(c_rows // sub - 1):
        e = (jb + 1) * sub
        r_ref = b[e - 1:e, :]
        qj = q * jnp.exp(jnp.minimum(b - r_ref, 0.0))
        kj = kk[jb * sub:e, :] * jnp.exp(r_ref - b[jb * sub:e, :])
        att = _dot_nt(qj.astype(BF16), kj.astype(BF16))
        att = jnp.where(row >= e, att, 0.0)
        o = o + _dot(att.astype(BF16), v16[jb * sub:e, :])
    zero_tile = jnp.zeros((tile, LANES), F32)
    q_parts, k_parts = [], []
    for r0 in range(0, c_rows, sub):
        r_ref = b[r0 + tile - 1:r0 + tile, :]
        first, second = slice(r0, r0 + tile), slice(r0 + tile, r0 + sub)
        k_parts += [kk[first, :] * jnp.exp(r_ref - b[first, :]), zero_tile]
        q_parts += [zero_tile, q[second, :] * jnp.exp(b[second, :] - r_ref)]
    att = _dot_nt(jnp.concatenate(q_parts, axis=0).astype(BF16), jnp.concatenate(k_parts, axis=0).astype(BF16))
    att = jnp.where(same_sub, att, 0.0)
    decay = None
    for dlt in range(tile):
        if dlt == 0:
            w = q * kk
        else:
            gate = _roll_in_tiles(f, dlt - 1)
            decay = gate if decay is None else decay * gate
            w = q * _roll_in_tiles(kk, dlt) * decay
        att = att + jnp.where(diag[dlt], jnp.sum(w, axis=-1, keepdims=True), 0.0)
    o = o + _dot(att.astype(BF16), v16)
    return _rms(o) * gn * _silu(g), st_new


def _hgrn_prompt_step(q_ref, f_ref, v_ref, g_ref, lbl_ref, gn_ref, y_ref, st_scr, group):
    c_rows = HGRN_CHUNK
    lb = _lower_bound([lbl_ref[l:l + 1, :] for l in range(lbl_ref.shape[0])])
    masks = _hgrn_masks()
    for r0 in range(0, q_ref.shape[0], c_rows):
        rs = slice(r0, r0 + c_rows)
        for gi in range(group):
            sl = slice(gi * LANES, (gi + 1) * LANES)
            y, st_new = _hgrn_chunk(q_ref[rs, sl], f_ref[rs, sl], v_ref[rs, sl], g_ref[rs, sl],
                                    lb[:, sl], gn_ref[...], st_scr[gi], masks)
            st_scr[gi] = st_new
            y_ref[rs, sl] = y


def _hgrn_sample_kernel(q_ref, f_ref, v_ref, g_ref, lbl_ref, gn_ref, s_ref, y_ref, so_ref,
                        *, n_heads):
    lb = _lower_bound([lbl_ref[l] for l in range(lbl_ref.shape[0])])
    pad = jnp.zeros((LANES - 3 * n_heads, LANES), F32)
    for s in range(q_ref.shape[0]):
        q = q_ref[s]
        v = v_ref[s]
        f = lb + (1.0 - lb) * _sigmoid(f_ref[s])
        kk = 1.0 - f
        cols = jnp.concatenate([f, kk, q, pad], axis=0).T
        o_rows = []
        for h in range(n_heads):
            f_col = cols[:, h:h + 1]
            k_col = cols[:, n_heads + h:n_heads + h + 1]
            q_col = cols[:, 2 * n_heads + h:2 * n_heads + h + 1]
            s_new = f_col * s_ref[s, h] + k_col * v[h:h + 1, :]
            so_ref[s, h] = s_new
            o_rows.append(jnp.sum(q_col * s_new, axis=0, keepdims=True))
        o = jnp.concatenate(o_rows, axis=0)
        y_ref[s] = _rms(o) * gn_ref[...] * _silu(g_ref[s])


def _hgrn_sample(zq4, lb_logits3, hg_norm, state, per_step=4):
    n_seq, n_heads = state.shape[0], state.shape[1]
    per_step = math.gcd(per_step, n_seq)

    def zspec(part):
        return pl.BlockSpec((None, per_step, n_heads, LANES), lambda b: (part, b, 0, 0))

    return pl.pallas_call(
        functools.partial(_hgrn_sample_kernel, n_heads=n_heads),
        grid=(n_seq // per_step,),
        in_specs=[zspec(0), zspec(1), zspec(2), zspec(3),
                  pl.BlockSpec(lb_logits3.shape, lambda b: (0, 0, 0)),
                  pl.BlockSpec((1, LANES), lambda b: (0, 0)),
                  pl.BlockSpec((per_step, n_heads, LANES, LANES), lambda b: (b, 0, 0, 0))],
        out_specs=[pl.BlockSpec((per_step, n_heads, LANES), lambda b: (b, 0, 0)),
                   pl.BlockSpec((per_step, n_heads, LANES, LANES), lambda b: (b, 0, 0, 0))],
        out_shape=[jax.ShapeDtypeStruct((n_seq, n_heads, LANES), F32),
                   jax.ShapeDtypeStruct(state.shape, F32)],
        compiler_params=_params("arbitrary"),
        name="hgrn_sample",
    )(zq4, zq4, zq4, zq4, lb_logits3, hg_norm, state)


def _attn_prompt_kernel(lq1_ref, lk1_ref, lq2_ref, lk2_ref, an_ref, q_ref, k_ref, v_ref, y_ref,
                        k16_scr, vt_scr, qm_scr, sa_scr, sb_scr, m_scr, l_scr, acc_scr, *, lam_init, tile,
                        qk_dim):
    n_tiles = vt_scr.shape[0]
    tq = tk = tile

    k16_scr[...] = k_ref[...].astype(BF16)
    for t in range(n_tiles):
        vt_scr[t] = v_ref[t * tk:(t + 1) * tk, :].T.astype(BF16)
    lam = _lam_value(lq1_ref[...], lk1_ref[...], lq2_ref[...], lk2_ref[...], lam_init)
    lane = lax.broadcasted_iota(jnp.int32, (1, LANES), 1)
    key = lax.broadcasted_iota(jnp.int32, (tk, 2 * tq), 0)
    qry = lax.broadcasted_iota(jnp.int32, (tk, 2 * tq), 1) % tq
    causal = key <= qry
    score_bufs = (sa_scr, sb_scr)

    for qi in range(n_tiles):
        par = qi % 2
        qm, m_s, l_s, acc = qm_scr.at[par], m_scr.at[par], l_scr.at[par], acc_scr.at[par]
        q = q_ref[qi * tq:(qi + 1) * tq, :] * (qk_dim ** -0.5 * LOG2_E)
        qm[:tq, :] = jnp.where(lane < qk_dim, q, 0.0).astype(BF16)
        qm[tq:, :] = jnp.where(lane >= qk_dim, q, 0.0).astype(BF16)
        m_s[...] = jnp.full(m_s.shape, -jnp.inf, F32)
        l_s[...] = jnp.zeros(l_s.shape, F32)
        acc[...] = jnp.zeros(acc.shape, F32)

        def scores(kj, dst, qm=qm):
            dst[...] = _dot_nt(k16_scr[kj * tk:(kj + 1) * tk, :], qm[...])

        def consume(kj, src, on_diagonal, m_s=m_s, l_s=l_s, acc=acc):
            st = src[...]
            if on_diagonal:
                st = jnp.where(causal, st, -jnp.inf)
            m_old = m_s[...]
            m_new = jnp.maximum(m_old, jnp.max(st, axis=0, keepdims=True))
            p = jnp.exp2(st - m_new)
            corr = jnp.exp2(m_old - m_new)
            l_s[...] = l_s[...] * corr + jnp.sum(p.reshape(tk // 8, 8, 2 * tq), axis=0)
            acc[...] = acc[...] * corr + _dot(vt_scr[kj], p.astype(BF16))
            m_s[...] = m_new

        scores(0, score_bufs[0])
        for kj in range(qi + 1):
            if kj < qi:
                scores(kj + 1, score_bufs[(kj + 1) % 2])
            consume(kj, score_bufs[kj % 2], kj == qi)

        o = acc[...] / jnp.sum(l_s[...], axis=0, keepdims=True)
        o = (o[:, :tq] - lam * o[:, tq:]).T
        y_ref[qi * tq:(qi + 1) * tq, :] = _rms(o) * an_ref[...] * (1.0 - lam_init)


def _attn_prompt(lam_params, at_norm, zq, k, v, n_batch, seq, n_heads, lam_init, tile=512):
    m = k.shape[0]
    nt = seq // tile
    qk_dim = lam_params[0].shape[1]
    small = [pl.BlockSpec(p.shape, lambda b, h: (0, 0)) for p in lam_params]
    seq_spec = pl.BlockSpec((seq, LANES), lambda b, h: (b, h))

    return pl.pallas_call(
        functools.partial(_attn_prompt_kernel, lam_init=lam_init, tile=tile, qk_dim=qk_dim),
        grid=(n_batch, n_heads),
        in_specs=small + [pl.BlockSpec((1, LANES), lambda b, h: (0, 0)),
                          pl.BlockSpec((None, seq, LANES), lambda b, h: (4, b, h)),
                          seq_spec, seq_spec],
        out_specs=seq_spec,
        out_shape=jax.ShapeDtypeStruct((m, n_heads * LANES), F32),
        scratch_shapes=[pltpu.VMEM((seq, LANES), BF16), pltpu.VMEM((nt, LANES, tile), BF16),
                        pltpu.VMEM((2, 2 * tile, LANES), BF16),
                        pltpu.VMEM((tile, 2 * tile), F32), pltpu.VMEM((tile, 2 * tile), F32),
                        pltpu.VMEM((2, 1, 2 * tile), F32),
                        pltpu.VMEM((2, 8, 2 * tile), F32), pltpu.VMEM((2, LANES, 2 * tile), F32)],
        compiler_params=_params("arbitrary", "arbitrary"),
        name="attn_prompt",
    )(*lam_params, at_norm, zq, k, v)


def _pages_hgrn_kernel(pt_ref, lq1_ref, lk1_ref, lq2_ref, lk2_ref, an_ref, q_ref, kn_ref, vn_ref,
                       z_ref, lbl_ref, gn_ref, *rest,
                       lam_init, n_heads, qk_dim, pages, group, sweep_steps):
    k_refs = rest[:pages]
    v_refs = rest[pages:2 * pages]
    y_ref, yh_ref, sfin_ref, m_scr, l_scr, acc_scr, st_scr = rest[2 * pages:]
    j = pl.program_id(1)
    page_rows = k_refs[0].shape[0] * n_heads
    sweep_pos = (pl.program_id(0) * pl.num_programs(1) + j) % sweep_steps

    @pl.when(sweep_pos == 0)
    def _():
        st_scr[...] = jnp.zeros_like(st_scr)

    q = q_ref[...] * (qk_dim ** -0.5)
    lane = lax.broadcasted_iota(jnp.int32, (1, LANES), 1)
    qm = jnp.concatenate([jnp.where(lane < qk_dim, q, 0.0), jnp.where(lane >= qk_dim, q, 0.0)], axis=0)

    @pl.when(j == 0)
    def _():
        kn = kn_ref[...]
        vn = vn_ref[...]
        m_scr[...] = jnp.sum(qm * jnp.concatenate([kn, kn], axis=0), axis=-1, keepdims=True)
        l_scr[...] = jnp.ones_like(l_scr)
        acc_scr[...] = jnp.concatenate([vn, vn], axis=0)

    qm16 = qm.astype(BF16)
    own_head = (lax.broadcasted_iota(jnp.int32, (2 * n_heads, page_rows), 1) % n_heads
                == lax.broadcasted_iota(jnp.int32, (2 * n_heads, page_rows), 0) % n_heads)
    _hgrn_prompt_step(z_ref.at[0], z_ref.at[1], z_ref.at[2], z_ref.at[3], lbl_ref, gn_ref, yh_ref, st_scr,
                      group)

    scores = [jnp.where(own_head, _dot_nt(qm16, k_refs[p][...].reshape(page_rows, LANES).astype(BF16)),
                        -jnp.inf) for p in range(pages)]
    m_old = m_scr[...]
    m_new = functools.reduce(jnp.maximum, [jnp.max(s, axis=-1, keepdims=True) for s in scores] + [m_old])
    corr = jnp.exp(m_old - m_new)
    l_new = l_scr[...] * corr
    acc = acc_scr[...] * corr
    for p in range(pages):
        pr = jnp.exp(scores[p] - m_new)
        l_new = l_new + jnp.sum(pr, axis=-1, keepdims=True)
        acc = acc + _dot(pr.astype(BF16), v_refs[p][...].reshape(page_rows, LANES).astype(BF16))
    l_scr[...] = l_new
    acc_scr[...] = acc
    m_scr[...] = m_new

    @pl.when(sweep_pos == sweep_steps - 1)
    def _():
        for gi in range(group):
            sfin_ref[gi] = st_scr[gi].T

    @pl.when(j == pl.num_programs(1) - 1)
    def _():
        lam = _lam_value(lq1_ref[...], lk1_ref[...], lq2_ref[...], lk2_ref[...], lam_init)
        o = acc_scr[...] / l_scr[...]
        w = o[:n_heads] - lam * o[n_heads:]
        y_ref[...] = _rms(w) * an_ref[...] * (1.0 - lam_init)


def _pages_and_hgrn(page_table, lam_params, at_norm, q3, kn3, vn3, cache_k, cache_v, lam_init,
                    zq, lb_logits, hg_norm, n_batch, seq, pages=16, group=8):
    n_seq, n_pages = page_table.shape
    _, _, page_size, n_heads, _ = cache_k.shape
    qk_dim = lam_params[0].shape[1]
    assert n_pages % pages == 0
    page_steps = n_pages // pages
    n_steps = n_seq * page_steps
    group = math.gcd(group, n_heads)
    n_groups = n_heads // group
    gw = group * LANES
    m = zq.shape[1]
    assert (n_batch * n_groups * seq) % (n_steps * HGRN_CHUNK) == 0
    rows = n_batch * n_groups * seq // n_steps
    assert seq % rows == 0
    sweep_steps = seq // rows

    small = [pl.BlockSpec(p.shape, lambda b, j, pt: (0, 0)) for p in lam_params]
    row_spec = pl.BlockSpec((None, n_heads, LANES), lambda b, j, pt: (b, 0, 0))

    def page_spec(p):
        return pl.BlockSpec(
            (None, None, page_size, n_heads, LANES),
            lambda b, j, pt: (0, pt[b * n_pages + j * pages + p], 0, 0, 0))

    def sweep(b, j):
        step = b * page_steps + j
        return step // (sweep_steps * n_groups), (step // sweep_steps) % n_groups, step % sweep_steps

    def rows_map(b, j, pt):
        sb, sg, sp = sweep(b, j)
        return (sb * sweep_steps + sp, sg)

    z_spec = pl.BlockSpec((4, rows, gw), lambda b, j, pt: (0,) + rows_map(b, j, pt))

    grid_spec = pltpu.PrefetchScalarGridSpec(
        num_scalar_prefetch=1,
        grid=(n_seq, page_steps),
        in_specs=(small + [pl.BlockSpec((1, LANES), lambda b, j, pt: (0, 0)), row_spec, row_spec, row_spec]
                  + [z_spec,
                     pl.BlockSpec((lb_logits.shape[0], gw), lambda b, j, pt: (0, sweep(b, j)[1])),
                     pl.BlockSpec((1, LANES), lambda b, j, pt: (0, 0))]
                  + [page_spec(p) for p in range(pages)] + [page_spec(p) for p in range(pages)]),
        out_specs=[row_spec,
                   pl.BlockSpec((rows, gw), rows_map),
                   pl.BlockSpec((None, group, LANES, LANES),
                                lambda b, j, pt: (sweep(b, j)[0], sweep(b, j)[1], 0, 0))],
        scratch_shapes=[pltpu.VMEM((2 * n_heads, 1), F32), pltpu.VMEM((2 * n_heads, 1), F32),
                        pltpu.VMEM((2 * n_heads, LANES), F32), pltpu.VMEM((group, LANES, LANES), F32)],
    )
    return pl.pallas_call(
        functools.partial(_pages_hgrn_kernel, lam_init=lam_init, n_heads=n_heads, qk_dim=qk_dim,
                          pages=pages, group=group, sweep_steps=sweep_steps),
        grid_spec=grid_spec,
        out_shape=[jax.ShapeDtypeStruct((n_seq, n_heads, LANES), F32),
                   jax.ShapeDtypeStruct((m, n_heads * LANES), F32),
                   jax.ShapeDtypeStruct((n_batch, n_heads, LANES, LANES), F32)],
        compiler_params=_params("arbitrary", "arbitrary"),
        name="pages_hgrn",
    )(page_table.reshape(-1), *lam_params, at_norm, q3, kn3, vn3, zq, lb_logits, hg_norm,
      *([cache_k] * pages), *([cache_v] * pages))


def _outproj_kernel(x_ref, yh_ref, ya_ref, g1_ref, sh_ref, sc_ref, n2_ref, w_ref, o_ref, h_ref):
    hw = yh_ref.shape[1]
    rows = min(256, x_ref.shape[0])
    for r in range(x_ref.shape[0] // rows):
        rs = slice(r * rows, (r + 1) * rows)
        acc = (_dot(yh_ref[rs, :].astype(BF16), w_ref[:hw, :])
               + _dot(ya_ref[rs, :].astype(BF16), w_ref[hw:, :]))
        x1 = x_ref[rs, :] + g1_ref[...] * acc
        o_ref[rs, :] = x1
        h_ref[rs, :] = (_rms(x1) * n2_ref[...] * (1.0 + sc_ref[...]) + sh_ref[...]).astype(BF16)


def _outproj(x2, y_hg, y_at, mod, mod_spec, norm2, w_out, tm):
    m, d = x2.shape
    return pl.pallas_call(
        _outproj_kernel,
        grid=(m // tm,),
        in_specs=[pl.BlockSpec((tm, d), lambda i: (i, 0)),
                  pl.BlockSpec((tm, y_hg.shape[1]), lambda i: (i, 0)),
                  pl.BlockSpec((tm, y_at.shape[1]), lambda i: (i, 0)),
                  mod_spec(2), mod_spec(3), mod_spec(4),
                  pl.BlockSpec((1, d), lambda i: (0, 0)),
                  pl.BlockSpec(w_out.shape, lambda i: (0, 0))],
        out_specs=[pl.BlockSpec((tm, d), lambda i: (i, 0)),
                   pl.BlockSpec((tm, d), lambda i: (i, 0))],
        out_shape=[jax.ShapeDtypeStruct((m, d), F32),
                   jax.ShapeDtypeStruct((m, d), BF16)],
        compiler_params=_params("arbitrary"),
        name="outproj",
    )(x2, y_hg, y_at, mod, mod, mod, norm2, w_out)


def _conv_gate(conv, u):
    return (_silu(conv) * u).astype(BF16)


def _ffn_up_kernel(h_ref, halo_ref, wa_ref, wu_ref, cw_ref, cb_ref, g_ref, tail_ref, h_scr, a_scr,
                   *, tm, tiles_per_seq, sub_blocks):
    i = pl.program_id(0)

    @pl.when(pl.program_id(1) == 0)
    def _():
        h_scr[:HALO, :] = halo_ref[...]
        h_scr[HALO:, :] = h_ref[...]

    rows = tm // sub_blocks
    for r in range(sub_blocks):
        lo = HALO + r * rows
        if r == 0:
            a_scr[:lo + rows, :] = _dot(h_scr[:lo + rows, :], wa_ref[...])

            @pl.when(i % tiles_per_seq == 0)
            def _():
                a_scr[:HALO, :] = jnp.zeros((HALO, a_scr.shape[1]), F32)
        else:
            a_scr[lo:lo + rows, :] = _dot(h_scr[lo:lo + rows, :], wa_ref[...])
        u = _dot(h_scr[lo:lo + rows, :], wu_ref[...])
        conv = (cb_ref[...] + a_scr[lo - 2:lo - 2 + rows, :] * cw_ref[0:1, :]
                + a_scr[lo - 1:lo - 1 + rows, :] * cw_ref[1:2, :]
                + a_scr[lo:lo + rows, :] * cw_ref[2:3, :])
        g_ref[r * rows:(r + 1) * rows, :] = _conv_gate(conv, u)
    tail_ref[...] = a_scr[HALO + tm - 2:, :]


def _ffn_up(h2, w_up, conv_w, conv_b, seq, tm=1024, tf=512, sub_blocks=4):
    m, d = h2.shape
    ff = w_up.shape[1] // 2
    nf = ff // tf
    tm = min(tm, seq)
    tps = seq // tm
    assert conv_w.shape[0] == 3 and ff % tf == 0 and seq % tm == 0
    g, tails = pl.pallas_call(
        functools.partial(_ffn_up_kernel, tm=tm, tiles_per_seq=tps, sub_blocks=sub_blocks),
        grid=(m // tm, nf),
        in_specs=[pl.BlockSpec((tm, d), lambda i, f: (i, 0)),
                  pl.BlockSpec((HALO, d), lambda i, f: (jnp.maximum(i * (tm // HALO) - 1, 0), 0)),
                  pl.BlockSpec((d, tf), lambda i, f: (0, f)),
                  pl.BlockSpec((d, tf), lambda i, f: (0, nf + f)),
                  pl.BlockSpec((3, tf), lambda i, f: (0, f)),
                  pl.BlockSpec((1, tf), lambda i, f: (0, f))],
        out_specs=[pl.BlockSpec((tm, tf), lambda i, f: (i, f)),
                   pl.BlockSpec((None, 2, tf), lambda i, f: (i, 0, f))],
        out_shape=[jax.ShapeDtypeStruct((m, ff), BF16),
                   jax.ShapeDtypeStruct((m // tm, 2, ff), F32)],
        scratch_shapes=[pltpu.VMEM((HALO + tm, d), BF16), pltpu.VMEM((HALO + tm, tf), F32)],
        compiler_params=_params("arbitrary", "arbitrary"),
        name="ffn_up",
    )(h2, h2, w_up, w_up, conv_w, conv_b)
    return g, tails[tps - 1::tps]


def _ffn_down_kernel(g_ref, x_ref, g2_ref, wd_ref, fn_ref, y_ref, ss_scr, *, tn, sub_blocks):
    n = pl.program_id(1)
    d = y_ref.shape[1]

    @pl.when(n == 0)
    def _():
        ss_scr[...] = jnp.zeros_like(ss_scr)

    rows = y_ref.shape[0] // sub_blocks
    for r in range(sub_blocks):
        rs = slice(r * rows, (r + 1) * rows)
        x2 = x_ref[rs, :] + g2_ref[...] * _dot(g_ref[rs, :], wd_ref[...])
        ss_scr[rs, :] += jnp.sum(x2 * x2, axis=-1, keepdims=True)
        y_ref[rs, pl.ds(pl.multiple_of(n * tn, tn), tn)] = x2

    @pl.when(n == d // tn - 1)
    def _():
        y_ref[...] = y_ref[...] * lax.rsqrt(ss_scr[...] / d + EPS) * fn_ref[...]


def _ffn_down(g, x1, mod_p, w_down, final_norm, seq, tm=1024, tn=256, sub_blocks=4):
    m, d = x1.shape
    ff = w_down.shape[0]
    tm = min(tm, seq)
    tps = seq // tm
    nd = d // tn
    assert d % tn == 0 and seq % tm == 0
    return pl.pallas_call(
        functools.partial(_ffn_down_kernel, tn=tn, sub_blocks=sub_blocks),
        grid=(m // tm, nd),
        in_specs=[pl.BlockSpec((tm, ff), lambda i, n: (i, 0)),
                  pl.BlockSpec((tm, tn), lambda i, n: (i, n)),
                  pl.BlockSpec((None, 1, tn), lambda i, n: (i // tps, 0, 5 * nd + n)),
                  pl.BlockSpec((ff, tn), lambda i, n: (0, n)),
                  pl.BlockSpec((1, d), lambda i, n: (0, 0))],
        out_specs=pl.BlockSpec((tm, d), lambda i, n: (i, 0)),
        out_shape=jax.ShapeDtypeStruct((m, d), F32),
        scratch_shapes=[pltpu.VMEM((tm, 1), F32)],
        compiler_params=_params("arbitrary", "arbitrary"),
        name="ffn_down",
    )(g, x1, mod_p, w_down, final_norm)


def _ffn_sample_kernel(x_ref, h_ref, g2_ref, p0_ref, p1_ref, wa_ref, wu_ref, cw_ref,
                       cb_ref, wd_ref, fn_ref, y_ref, a_ref, acc_scr):
    f = pl.program_id(0)

    @pl.when(f == 0)
    def _():
        acc_scr[...] = jnp.zeros_like(acc_scr)

    a = _dot(h_ref[...], wa_ref[...])
    u = _dot(h_ref[...], wu_ref[...])
    conv = (cb_ref[...] + p0_ref[...] * cw_ref[0:1, :] + p1_ref[...] * cw_ref[1:2, :]
            + a * cw_ref[2:3, :])
    acc_scr[...] += _dot(_conv_gate(conv, u), wd_ref[...])
    a_ref[...] = a

    @pl.when(f == pl.num_programs(0) - 1)
    def _():
        y_ref[...] = _rms(x_ref[...] + g2_ref[...] * acc_scr[...]) * fn_ref[...]


def _ffn_sample(x1, h2, mod_s, prev2, w_up, conv_w, conv_b, w_down, final_norm, tf=512):
    m, d = x1.shape
    ff = w_down.shape[0]
    nf = ff // tf

    return pl.pallas_call(
        _ffn_sample_kernel,
        grid=(nf,),
        in_specs=[pl.BlockSpec((m, d), lambda f: (0, 0)),
                  pl.BlockSpec((m, d), lambda f: (0, 0)),
                  pl.BlockSpec((m, d), lambda f: (0, 5)),
                  pl.BlockSpec((m, tf), lambda f: (0, f)),
                  pl.BlockSpec((m, tf), lambda f: (0, nf + f)),
                  pl.BlockSpec((d, tf), lambda f: (0, f)),
                  pl.BlockSpec((d, tf), lambda f: (0, nf + f)),
                  pl.BlockSpec((3, tf), lambda f: (0, f)),
                  pl.BlockSpec((1, tf), lambda f: (0, f)),
                  pl.BlockSpec((tf, d), lambda f: (f, 0)),
                  pl.BlockSpec((1, d), lambda f: (0, 0))],
        out_specs=[pl.BlockSpec((m, d), lambda f: (0, 0)),
                   pl.BlockSpec((m, tf), lambda f: (0, f))],
        out_shape=[jax.ShapeDtypeStruct((m, d), F32),
                   jax.ShapeDtypeStruct((m, ff), F32)],
        scratch_shapes=[pltpu.VMEM((m, d), F32)],
        compiler_params=_params("arbitrary"),
        name="ffn_sample",
    )(x1, h2, mod_s, prev2, prev2, w_up, w_up, conv_w, conv_b, w_down, final_norm)


def _rope_tables(pos, qk_dim):
    rot = qk_dim // 4
    half = rot // 2
    inv_freq = ROPE_THETA ** (-jnp.arange(half, dtype=F32) / half)
    ang = pos.astype(F32)[:, None] * inv_freq[None, :]
    cos, sin = jnp.cos(ang), jnp.sin(ang)
    ones = jnp.ones((pos.shape[0], qk_dim - rot), F32)
    zeros = jnp.zeros((pos.shape[0], qk_dim - rot), F32)
    zh = jnp.zeros_like(sin)
    cos_t = jnp.concatenate([cos, cos, ones] * 2, axis=1)
    sin_a = jnp.concatenate([zh, sin, zeros] * 2, axis=1)
    sin_b = jnp.concatenate([-sin, zh, zeros] * 2, axis=1)
    return cos_t, sin_a, sin_b


def kernel(x_prompt, x_sample, c_prompt, c_sample, cache_k, cache_v, state_hgrn, state_conv, page_table,
           w_ada, b_ada, norm1, norm2, w_in, hg_lb_logits, hg_norm, lam_q1, lam_k1, lam_q2, lam_k2,
           at_norm, w_out, w_up, conv_w, conv_b, w_down, final_norm):
    n_batch, seq, d = x_prompt.shape
    n_seq = x_sample.shape[0]
    depth = w_in.shape[0]
    assert depth == 1 and x_sample.shape[1] == 1
    n_heads = state_hgrn.shape[2]
    assert cache_k.shape[3] == n_heads and cache_k.shape[4] == LANES and state_hgrn.shape[3] == LANES
    qk_dim = lam_q1.shape[1]
    past_len = page_table.shape[1] * cache_k.shape[2]
    lam_init = 0.8 - 0.6 * math.exp(-0.3 * 0)
    ff = w_down.shape[1]
    width = n_heads * LANES
    lam_params = [lam_q1, lam_k1, lam_q2, lam_k2]

    w_in16 = w_in[0].astype(BF16)
    w_out16 = w_out[0].astype(BF16)
    w_up16 = w_up[0].astype(BF16)
    w_down16 = w_down[0].astype(BF16)

    mod = _adaln(jnp.concatenate([c_prompt, c_sample], axis=0), w_ada[0], b_ada[0])
    mod_p = mod[:n_batch].reshape(n_batch, 1, N_MOD * d)
    mod_s = mod[n_batch:]

    tm_in = min(1024, seq)
    tps_in = seq // tm_in
    tm_out = min(512, seq)
    tps_out = seq // tm_out
    xp = x_prompt.reshape(n_batch * seq, d)
    tables_p = _rope_tables(jnp.arange(seq, dtype=jnp.int32), qk_dim)

    def pspec(part):
        return pl.BlockSpec((None, 1, d), lambda i, j: (i // tps_in, 0, part))

    zq, k_p, v_p = _inproj(xp, mod_p, mod_p, (pspec(0), pspec(1)), norm1, w_in16, tables_p,
                           pl.BlockSpec((tm_in, LANES), lambda i, j: (i % tps_in, 0)), tm_in, n_heads)

    xs = x_sample.reshape(n_seq, d)
    tables_s = _rope_tables(jnp.full((n_seq,), past_len, jnp.int32), qk_dim)

    def sspec(part):
        return pl.BlockSpec((n_seq, d), lambda i, j: (0, part))

    zq_s, k_s, v_s = _inproj(xs, mod_s, mod_s, (sspec(0), sspec(1)), norm1, w_in16, tables_s,
                             pl.BlockSpec((n_seq, LANES), lambda i, j: (0, 0)), n_seq, n_heads, split=1)
    zq_s4 = zq_s.reshape(5, n_seq, n_heads, LANES)

    y_at_s, y_hg, hgrn_p = _pages_and_hgrn(
        page_table, lam_params, at_norm, zq_s4[4], k_s.reshape(n_seq, n_heads, LANES),
        v_s.reshape(n_seq, n_heads, LANES), cache_k, cache_v, lam_init,
        zq, hg_lb_logits, hg_norm, n_batch, seq)

    y_at = _attn_prompt(lam_params, at_norm, zq, k_p, v_p, n_batch, seq, n_heads, lam_init)
    x1, h2 = _outproj(xp, y_hg, y_at, mod_p,
                      lambda part: pl.BlockSpec((None, 1, d), lambda i: (i // tps_out, 0, part)),
                      norm2, w_out16, tm_out)
    g_p, conv_p = _ffn_up(h2, w_up16, conv_w[0], conv_b, seq)
    y_p = _ffn_down(g_p, x1, mod_p, w_down16, final_norm.reshape(1, d), seq)

    y_hg_s, hgrn_s = _hgrn_sample(zq_s4, hg_lb_logits.reshape(-1, n_heads, LANES), hg_norm, state_hgrn[0])
    x1_s, h2_s = _outproj(xs, y_hg_s.reshape(n_seq, width), y_at_s.reshape(n_seq, width), mod_s,
                          lambda part: pl.BlockSpec((n_seq, d), lambda i: (0, part)),
                          norm2, w_out16, n_seq)
    prev = state_conv[0]
    y_s, a_s = _ffn_sample(x1_s, h2_s, mod_s, prev.reshape(n_seq, 2 * ff), w_up16, conv_w[0], conv_b,
                           w_down16, final_norm.reshape(1, d))
    conv_s = jnp.stack([prev[:, 1, :], a_s], axis=1)

    return (y_p.reshape(n_batch, seq, d), y_s.reshape(n_seq, 1, d),
            k_p.reshape(1, n_batch, seq, n_heads, LANES), v_p.reshape(1, n_batch, seq, n_heads, LANES),
            k_s.reshape(1, n_seq, 1, n_heads, LANES), v_s.reshape(1, n_seq, 1, n_heads, LANES),
            hgrn_p[None], hgrn_s[None], conv_p[None], conv_s[None])
```

```python
import functools
import math

import jax
import jax.numpy as jnp
from jax import lax
from jax.experimental import pallas as pl
from jax.experimental.pallas import tpu as pltpu

EPS = 1e-6
ROPE_THETA = 500000.0
LOG2_E = math.log2(math.e)
N_MOD = 6
LANES = 128
HGRN_CHUNK = 64
HGRN_SUB = 16
HGRN_TILE = 8
HALO = 16
VMEM_LIMIT_BYTES = 56 * 1024 * 1024

F32 = jnp.float32
BF16 = jnp.bfloat16


def _params(*sem):
    return pltpu.CompilerParams(dimension_semantics=sem, vmem_limit_bytes=VMEM_LIMIT_BYTES)


def _silu(x):
    return x / (1.0 + jnp.exp(-x))


def _sigmoid(x):
    return 1.0 / (1.0 + jnp.exp(-x))


def _rms(x):
    return x * lax.rsqrt(jnp.mean(x * x, axis=-1, keepdims=True) + EPS)


def _dot(a, b):
    return jnp.dot(a, b, preferred_element_type=F32)


def _dot_nt(a, b):
    return lax.dot_general(a, b, (((1,), (1,)), ((), ())), preferred_element_type=F32)


def _lam_value(lq1, lk1, lq2, lk2, lam_init):
    return (jnp.exp(jnp.sum(lq1 * lk1, axis=-1, keepdims=True))
            - jnp.exp(jnp.sum(lq2 * lk2, axis=-1, keepdims=True)) + lam_init)


def _lower_bound(layer_logits):
    mx = functools.reduce(jnp.maximum, layer_logits)
    es = [jnp.exp(r - mx) for r in layer_logits]
    return es[0] / functools.reduce(jnp.add, es)


def _adaln_kernel(c_ref, w_ref, b_ref, o_ref):
    s = _silu(c_ref[...]).astype(BF16)
    o_ref[...] = _dot(s, w_ref[...].astype(BF16)) + b_ref[...]


def _adaln(c_all, w_ada, b_ada, tn=512):
    rows, d = c_all.shape
    n = w_ada.shape[1]
    assert n % tn == 0
    return pl.pallas_call(
        _adaln_kernel,
        grid=(n // tn,),
        in_specs=[pl.BlockSpec((rows, d), lambda j: (0, 0)),
                  pl.BlockSpec((d, tn), lambda j: (0, j)),
                  pl.BlockSpec((1, tn), lambda j: (0, j))],
        out_specs=pl.BlockSpec((rows, tn), lambda j: (0, j)),
        out_shape=jax.ShapeDtypeStruct((rows, n), F32),
        compiler_params=_params("arbitrary"),
        name="adaln",
    )(c_all, w_ada, b_ada.reshape(1, n))


def _rope(blk, cos_t, sin_a, sin_b):
    return (blk * cos_t + pltpu.roll(blk, 8, axis=1) * sin_a
            + pltpu.roll(blk, LANES - 8, axis=1) * sin_b)


def _inproj_kernel(x_ref, sh_ref, sc_ref, g_ref, w_ref, cos_ref, sa_ref, sb_ref,
                   zq_ref, k_ref, v_ref, h_scr, *, split, sub_blocks):
    j = pl.program_id(1)
    part = j // split

    tm, tn = zq_ref.shape
    rows = tm // sub_blocks

    def rows_of(ref, rs):
        return ref[...] if ref.shape[0] == 1 else ref[rs, :]

    def project(dst, rotary=False, modulate=False):
        for r in range(sub_blocks):
            rs = slice(r * rows, (r + 1) * rows)
            if modulate:
                h = _rms(x_ref[rs, :]) * g_ref[...] * (1.0 + rows_of(sc_ref, rs)) + rows_of(sh_ref, rs)
                h_scr[rs, :] = h.astype(BF16)
            acc = _dot(h_scr[rs, :], w_ref[...])
            if rotary:
                for hh in range(tn // LANES):
                    sl = slice(hh * LANES, (hh + 1) * LANES)
                    dst[rs, sl] = _rope(acc[:, sl], cos_ref[rs, :], sa_ref[rs, :], sb_ref[rs, :])
            else:
                dst[rs, :] = acc

    @pl.when(j == 0)
    def _():
        project(zq_ref, modulate=True)

    @pl.when((j > 0) & (part < 4))
    def _():
        project(zq_ref)

    @pl.when(part == 4)
    def _():
        project(zq_ref, rotary=True)

    @pl.when(part == 5)
    def _():
        project(k_ref, rotary=True)

    @pl.when(part == 6)
    def _():
        project(v_ref)


def _inproj(x2, sh, sc, mod_spec, norm_g, w_in, tables, tab_spec, tm, n_heads, split=2):
    m, d = x2.shape
    width = n_heads * LANES
    assert w_in.shape[1] == 7 * width and n_heads % split == 0
    tn = width // split
    sh_spec, sc_spec = mod_spec

    def zq_map(i, j):
        jz = jnp.minimum(j, 5 * split - 1)
        return (jz // split, i, jz % split)

    return pl.pallas_call(
        functools.partial(_inproj_kernel, split=split, sub_blocks=max(1, tm // 256)),
        grid=(m // tm, 7 * split),
        in_specs=[pl.BlockSpec((tm, d), lambda i, j: (i, 0)),
                  sh_spec, sc_spec,
                  pl.BlockSpec((1, d), lambda i, j: (0, 0)),
                  pl.BlockSpec((d, tn), lambda i, j: (0, j)),
                  tab_spec, tab_spec, tab_spec],
        out_specs=[pl.BlockSpec((None, tm, tn), zq_map),
                   pl.BlockSpec((tm, tn), lambda i, j: (i, jnp.clip(j - 5 * split, 0, split - 1))),
                   pl.BlockSpec((tm, tn), lambda i, j: (i, jnp.clip(j - 6 * split, 0, split - 1)))],
        out_shape=[jax.ShapeDtypeStruct((5, m, width), F32),
                   jax.ShapeDtypeStruct((m, width), F32),
                   jax.ShapeDtypeStruct((m, width), F32)],
        scratch_shapes=[pltpu.VMEM((tm, d), BF16)],
        compiler_params=_params("arbitrary", "arbitrary"),
        name="inproj",
    )(x2, sh, sc, norm_g, w_in, *tables)


def _roll_in_tiles(x, shift):
    if shift == 0:
        return x
    tiles = [pltpu.roll(x[r:r + HGRN_TILE, :], shift, axis=0) for r in range(0, x.shape[0], HGRN_TILE)]
    return jnp.concatenate(tiles, axis=0)


def _hgrn_masks():
    c_rows = HGRN_CHUNK
    ri = lax.broadcasted_iota(jnp.int32, (c_rows, c_rows), 0)
    ci = lax.broadcasted_iota(jnp.int32, (c_rows, c_rows), 1)
    tri = jnp.where(ri >= ci, 1.0, 0.0).astype(BF16)
    same_sub = ri // HGRN_SUB == ci // HGRN_SUB
    same_tile = ri // HGRN_TILE == ci // HGRN_TILE
    diag = [same_tile & (ci == ri - dlt) for dlt in range(HGRN_TILE)]
    row = lax.broadcasted_iota(jnp.int32, (c_rows, 1), 0)
    return tri, same_sub, diag, row


def _hgrn_chunk(q, hf, v, g, lb, gn, st, masks):
    c_rows, sub, tile = HGRN_CHUNK, HGRN_SUB, HGRN_TILE
    tri, same_sub, diag, row = masks
    f = lb + (1.0 - lb) * _sigmoid(hf)
    kk = 1.0 - f
    lf = jnp.log(f)
    hi = lf.astype(BF16)
    r1 = lf - hi.astype(F32)
    mid = r1.astype(BF16)
    lo = (r1 - mid.astype(F32)).astype(BF16)
    b = _dot(tri, hi) + _dot(tri, mid) + _dot(tri, lo)
    b_last = b[c_rows - 1:c_rows, :]

    o = _dot_nt((q * jnp.exp(b)).astype(BF16), st.astype(BF16))
    kd = kk * jnp.exp(b_last - b)
    st_new = st * jnp.exp(b_last) + _dot(v.T.astype(BF16), kd.astype(BF16))

    v16 = v.astype(BF16)
    for jb in range(c_rows // sub - 1):
        e = (jb + 1) * sub
        r_ref = b[e - 1:e, :]
        qj = q * jnp.exp(jnp.minimum(b - r_ref, 0.0))
        kj = kk[jb * sub:e, :] * jnp.exp(r_ref - b[jb * sub:e, :])
        att = _dot_nt(qj.astype(BF16), kj.astype(BF16))
        att = jnp.where(row >= e, att, 0.0)
        o = o + _dot(att.astype(BF16), v16[jb * sub:e, :])
    zero_tile = jnp.zeros((tile, LANES), F32)
    q_parts, k_parts = [], []
    for r0 in range(0, c_rows, sub):
        r_ref = b[r0 + tile - 1:r0 + tile, :]
        first, second = slice(r0, r0 + tile), slice(r0 + tile, r0 + sub)
        k_parts += [kk[first, :] * jnp.exp(r_ref - b[first, :]), zero_tile]
        q_parts += [zero_tile, q[second, :] * jnp.exp(b[second, :] - r_ref)]
    att = _dot_nt(jnp.concatenate(q_parts, axis=0).astype(BF16), jnp.concatenate(k_parts, axis=0).astype(BF16))
    att = jnp.where(same_sub, att, 0.0)
    decay = None
    for dlt in range(tile):
        if dlt == 0:
            w = q * kk
        else:
            gate = _roll_in_tiles(f, dlt - 1)
            decay = gate if decay is None else decay * gate
            w = q * _roll_in_tiles(kk, dlt) * decay
        att = att + jnp.where(diag[dlt], jnp.sum(w, axis=-1, keepdims=True), 0.0)
    o = o + _dot(att.astype(BF16), v16)
    return _rms(o) * gn * _silu(g), st_new


def _hgrn_prompt_step(q_ref, f_ref, v_ref, g_ref, lbl_ref, gn_ref, y_ref, st_scr, group):
    c_rows = HGRN_CHUNK
    lb = _lower_bound([lbl_ref[l:l + 1, :] for l in range(lbl_ref.shape[0])])
    masks = _hgrn_masks()
    for r0 in range(0, q_ref.shape[0], c_rows):
        rs = slice(r0, r0 + c_rows)
        for gi in range(group):
            sl = slice(gi * LANES, (gi + 1) * LANES)
            y, st_new = _hgrn_chunk(q_ref[rs, sl], f_ref[rs, sl], v_ref[rs, sl], g_ref[rs, sl],
                                    lb[:, sl], gn_ref[...], st_scr[gi], masks)
            st_scr[gi] = st_new
            y_ref[rs, sl] = y


def _hgrn_sample_kernel(q_ref, f_ref, v_ref, g_ref, lbl_ref, gn_ref, s_ref, y_ref, so_ref,
                        *, n_heads):
    lb = _lower_bound([lbl_ref[l] for l in range(lbl_ref.shape[0])])
    pad = jnp.zeros((LANES - 3 * n_heads, LANES), F32)
    for s in range(q_ref.shape[0]):
        q = q_ref[s]
        v = v_ref[s]
        f = lb + (1.0 - lb) * _sigmoid(f_ref[s])
        kk = 1.0 - f
        cols = jnp.concatenate([f, kk, q, pad], axis=0).T
        o_rows = []
        for h in range(n_heads):
            f_col = cols[:, h:h + 1]
            k_col = cols[:, n_heads + h:n_heads + h + 1]
            q_col = cols[:, 2 * n_heads + h:2 * n_heads + h + 1]
            s_new = f_col * s_ref[s, h] + k_col * v[h:h + 1, :]
            so_ref[s, h] = s_new
            o_rows.append(jnp.sum(q_col * s_new, axis=0, keepdims=True))
        o = jnp.concatenate(o_rows, axis=0)
        y_ref[s] = _rms(o) * gn_ref[...] * _silu(g_ref[s])


def _hgrn_sample(zq4, lb_logits3, hg_norm, state, per_step=4):
    n_seq, n_heads = state.shape[0], state.shape[1]
    per_step = math.gcd(per_step, n_seq)

    def zspec(part):
        return pl.BlockSpec((None, per_step, n_heads, LANES), lambda b: (part, b, 0, 0))

    return pl.pallas_call(
        functools.partial(_hgrn_sample_kernel, n_heads=n_heads),
        grid=(n_seq // per_step,),
        in_specs=[zspec(0), zspec(1), zspec(2), zspec(3),
                  pl.BlockSpec(lb_logits3.shape, lambda b: (0, 0, 0)),
                  pl.BlockSpec((1, LANES), lambda b: (0, 0)),
                  pl.BlockSpec((per_step, n_heads, LANES, LANES), lambda b: (b, 0, 0, 0))],
        out_specs=[pl.BlockSpec((per_step, n_heads, LANES), lambda b: (b, 0, 0)),
                   pl.BlockSpec((per_step, n_heads, LANES, LANES), lambda b: (b, 0, 0, 0))],
        out_shape=[jax.ShapeDtypeStruct((n_seq, n_heads, LANES), F32),
                   jax.ShapeDtypeStruct(state.shape, F32)],
        compiler_params=_params("arbitrary"),
        name="hgrn_sample",
    )(zq4, zq4, zq4, zq4, lb_logits3, hg_norm, state)


def _attn_prompt_kernel(lq1_ref, lk1_ref, lq2_ref, lk2_ref, an_ref, q_ref, k_ref, v_ref, y_ref,
                        k16_scr, vt_scr, qm_scr, sa_scr, sb_scr, m_scr, l_scr, acc_scr, *, lam_init, tile,
                        qk_dim):
    n_tiles = vt_scr.shape[0]
    tq = tk = tile

    k16_scr[...] = k_ref[...].astype(BF16)
    for t in range(n_tiles):
        vt_scr[t] = v_ref[t * tk:(t + 1) * tk, :].T.astype(BF16)
    lam = _lam_value(lq1_ref[...], lk1_ref[...], lq2_ref[...], lk2_ref[...], lam_init)
    lane = lax.broadcasted_iota(jnp.int32, (1, LANES), 1)
    key = lax.broadcasted_iota(jnp.int32, (tk, 2 * tq), 0)
    qry = lax.broadcasted_iota(jnp.int32, (tk, 2 * tq), 1) % tq
    causal = key <= qry
    score_bufs = (sa_scr, sb_scr)

    for qi in range(n_tiles):
        par = qi % 2
        qm, m_s, l_s, acc = qm_scr.at[par], m_scr.at[par], l_scr.at[par], acc_scr.at[par]
        q = q_ref[qi * tq:(qi + 1) * tq, :] * (qk_dim ** -0.5 * LOG2_E)
        qm[:tq, :] = jnp.where(lane < qk_dim, q, 0.0).astype(BF16)
        qm[tq:, :] = jnp.where(lane >= qk_dim, q, 0.0).astype(BF16)
        m_s[...] = jnp.full(m_s.shape, -jnp.inf, F32)
        l_s[...] = jnp.zeros(l_s.shape, F32)
        acc[...] = jnp.zeros(acc.shape, F32)

        def scores(kj, dst, qm=qm):
            dst[...] = _dot_nt(k16_scr[kj * tk:(kj + 1) * tk, :], qm[...])

        def consume(kj, src, on_diagonal, m_s=m_s, l_s=l_s, acc=acc):
            for c in range(2):
                cs = slice(c * tq, (c + 1) * tq)
                st = src[:, cs]
                if on_diagonal:
                    st = jnp.where(causal[:, cs], st, -jnp.inf)
                m_old = m_s[:, cs]
                m_new = jnp.maximum(m_old, jnp.max(st, axis=0, keepdims=True))
                p = jnp.exp2(st - m_new)
                corr = jnp.exp2(m_old - m_new)
                l_s[:, cs] = l_s[:, cs] * corr + jnp.sum(p.reshape(tk // 8, 8, tq), axis=0)
                acc[:, cs] = acc[:, cs] * corr + _dot(vt_scr[kj], p.astype(BF16))
                m_s[:, cs] = m_new

        scores(0, score_bufs[0])
        for kj in range(qi + 1):
            if kj < qi:
                scores(kj + 1, score_bufs[(kj + 1) % 2])
            consume(kj, score_bufs[kj % 2], kj == qi)

        o = acc[...] / jnp.sum(l_s[...], axis=0, keepdims=True)
        o = (o[:, :tq] - lam * o[:, tq:]).T
        y_ref[qi * tq:(qi + 1) * tq, :] = _rms(o) * an_ref[...] * (1.0 - lam_init)


def _attn_prompt(lam_params, at_norm, zq, k, v, n_batch, seq, n_heads, lam_init, tile=512):
    m = k.shape[0]
    nt = seq // tile
    qk_dim = lam_params[0].shape[1]
    small = [pl.BlockSpec(p.shape, lambda b, h: (0, 0)) for p in lam_params]
    seq_spec = pl.BlockSpec((seq, LANES), lambda b, h: (b, h))

    return pl.pallas_call(
        functools.partial(_attn_prompt_kernel, lam_init=lam_init, tile=tile, qk_dim=qk_dim),
        grid=(n_batch, n_heads),
        in_specs=small + [pl.BlockSpec((1, LANES), lambda b, h: (0, 0)),
                          pl.BlockSpec((None, seq, LANES), lambda b, h: (4, b, h)),
                          seq_spec, seq_spec],
        out_specs=seq_spec,
        out_shape=jax.ShapeDtypeStruct((m, n_heads * LANES), F32),
        scratch_shapes=[pltpu.VMEM((seq, LANES), BF16), pltpu.VMEM((nt, LANES, tile), BF16),
                        pltpu.VMEM((2, 2 * tile, LANES), BF16),
                        pltpu.VMEM((tile, 2 * tile), F32), pltpu.VMEM((tile, 2 * tile), F32),
                        pltpu.VMEM((2, 1, 2 * tile), F32),
                        pltpu.VMEM((2, 8, 2 * tile), F32), pltpu.VMEM((2, LANES, 2 * tile), F32)],
        compiler_params=_params("arbitrary", "arbitrary"),
        name="attn_prompt",
    )(*lam_params, at_norm, zq, k, v)


def _pages_hgrn_kernel(pt_ref, lq1_ref, lk1_ref, lq2_ref, lk2_ref, an_ref, q_ref, kn_ref, vn_ref,
                       z_ref, lbl_ref, gn_ref, *rest,
                       lam_init, n_heads, qk_dim, pages, group, sweep_steps):
    k_refs = rest[:pages]
    v_refs = rest[pages:2 * pages]
    y_ref, yh_ref, sfin_ref, m_scr, l_scr, acc_scr, st_scr = rest[2 * pages:]
    j = pl.program_id(1)
    page_rows = k_refs[0].shape[0] * n_heads
    sweep_pos = (pl.program_id(0) * pl.num_programs(1) + j) % sweep_steps

    @pl.when(sweep_pos == 0)
    def _():
        st_scr[...] = jnp.zeros_like(st_scr)

    q = q_ref[...] * (qk_dim ** -0.5)
    lane = lax.broadcasted_iota(jnp.int32, (1, LANES), 1)
    qm = jnp.concatenate([jnp.where(lane < qk_dim, q, 0.0), jnp.where(lane >= qk_dim, q, 0.0)], axis=0)

    @pl.when(j == 0)
    def _():
        kn = kn_ref[...]
        vn = vn_ref[...]
        m_scr[...] = jnp.sum(qm * jnp.concatenate([kn, kn], axis=0), axis=-1, keepdims=True)
        l_scr[...] = jnp.ones_like(l_scr)
        acc_scr[...] = jnp.concatenate([vn, vn], axis=0)

    qm16 = qm.astype(BF16)
    own_head = (lax.broadcasted_iota(jnp.int32, (2 * n_heads, page_rows), 1) % n_heads
                == lax.broadcasted_iota(jnp.int32, (2 * n_heads, page_rows), 0) % n_heads)
    _hgrn_prompt_step(z_ref.at[0], z_ref.at[1], z_ref.at[2], z_ref.at[3], lbl_ref, gn_ref, yh_ref, st_scr,
                      group)

    scores = [jnp.where(own_head, _dot_nt(qm16, k_refs[p][...].reshape(page_rows, LANES).astype(BF16)),
                        -jnp.inf) for p in range(pages)]
    m_old = m_scr[...]
    m_new = functools.reduce(jnp.maximum, [jnp.max(s, axis=-1, keepdims=True) for s in scores] + [m_old])
    corr = jnp.exp(m_old - m_new)
    l_new = l_scr[...] * corr
    acc = acc_scr[...] * corr
    for p in range(pages):
        pr = jnp.exp(scores[p] - m_new)
        l_new = l_new + jnp.sum(pr, axis=-1, keepdims=True)
        acc = acc + _dot(pr.astype(BF16), v_refs[p][...].reshape(page_rows, LANES).astype(BF16))
    l_scr[...] = l_new
    acc_scr[...] = acc
    m_scr[...] = m_new

    @pl.when(sweep_pos == sweep_steps - 1)
    def _():
        for gi in range(group):
            sfin_ref[gi] = st_scr[gi].T

    @pl.when(j == pl.num_programs(1) - 1)
    def _():
        lam = _lam_value(lq1_ref[...], lk1_ref[...], lq2_ref[...], lk2_ref[...], lam_init)
        o = acc_scr[...] / l_scr[...]
        w = o[:n_heads] - lam * o[n_heads:]
        y_ref[...] = _rms(w) * an_ref[...] * (1.0 - lam_init)


def _pages_and_hgrn(page_table, lam_params, at_norm, q3, kn3, vn3, cache_k, cache_v, lam_init,
                    zq, lb_logits, hg_norm, n_batch, seq, pages=16, group=8):
    n_seq, n_pages = page_table.shape
    _, _, page_size, n_heads, _ = cache_k.shape
    qk_dim = lam_params[0].shape[1]
    assert n_pages % pages == 0
    page_steps = n_pages // pages
    n_steps = n_seq * page_steps
    group = math.gcd(group, n_heads)
    n_groups = n_heads // group
    gw = group * LANES
    m = zq.shape[1]
    assert (n_batch * n_groups * seq) % (n_steps * HGRN_CHUNK) == 0
    rows = n_batch * n_groups * seq // n_steps
    assert seq % rows == 0
    sweep_steps = seq // rows

    small = [pl.BlockSpec(p.shape, lambda b, j, pt: (0, 0)) for p in lam_params]
    row_spec = pl.BlockSpec((None, n_heads, LANES), lambda b, j, pt: (b, 0, 0))

    def page_spec(p):
        return pl.BlockSpec(
            (None, None, page_size, n_heads, LANES),
            lambda b, j, pt: (0, pt[b * n_pages + j * pages + p], 0, 0, 0))

    def sweep(b, j):
        step = b * page_steps + j
        return step // (sweep_steps * n_groups), (step // sweep_steps) % n_groups, step % sweep_steps

    def rows_map(b, j, pt):
        sb, sg, sp = sweep(b, j)
        return (sb * sweep_steps + sp, sg)

    z_spec = pl.BlockSpec((4, rows, gw), lambda b, j, pt: (0,) + rows_map(b, j, pt))

    grid_spec = pltpu.PrefetchScalarGridSpec(
        num_scalar_prefetch=1,
        grid=(n_seq, page_steps),
        in_specs=(small + [pl.BlockSpec((1, LANES), lambda b, j, pt: (0, 0)), row_spec, row_spec, row_spec]
                  + [z_spec,
                     pl.BlockSpec((lb_logits.shape[0], gw), lambda b, j, pt: (0, sweep(b, j)[1])),
                     pl.BlockSpec((1, LANES), lambda b, j, pt: (0, 0))]
                  + [page_spec(p) for p in range(pages)] + [page_spec(p) for p in range(pages)]),
        out_specs=[row_spec,
                   pl.BlockSpec((rows, gw), rows_map),
                   pl.BlockSpec((None, group, LANES, LANES),
                                lambda b, j, pt: (sweep(b, j)[0], sweep(b, j)[1], 0, 0))],
        scratch_shapes=[pltpu.VMEM((2 * n_heads, 1), F32), pltpu.VMEM((2 * n_heads, 1), F32),
                        pltpu.VMEM((2 * n_heads, LANES), F32), pltpu.VMEM((group, LANES, LANES), F32)],
    )
    return pl.pallas_call(
        functools.partial(_pages_hgrn_kernel, lam_init=lam_init, n_heads=n_heads, qk_dim=qk_dim,
                          pages=pages, group=group, sweep_steps=sweep_steps),
        grid_spec=grid_spec,
        out_shape=[jax.ShapeDtypeStruct((n_seq, n_heads, LANES), F32),
                   jax.ShapeDtypeStruct((m, n_heads * LANES), F32),
                   jax.ShapeDtypeStruct((n_batch, n_heads, LANES, LANES), F32)],
        compiler_params=_params("arbitrary", "arbitrary"),
        name="pages_hgrn",
    )(page_table.reshape(-1), *lam_params, at_norm, q3, kn3, vn3, zq, lb_logits, hg_norm,
      *([cache_k] * pages), *([cache_v] * pages))


def _outproj_kernel(x_ref, yh_ref, ya_ref, g1_ref, sh_ref, sc_ref, n2_ref, w_ref, o_ref, h_ref):
    hw = yh_ref.shape[1]
    rows = min(256, x_ref.shape[0])
    for r in range(x_ref.shape[0] // rows):
        rs = slice(r * rows, (r + 1) * rows)
        acc = (_dot(yh_ref[rs, :].astype(BF16), w_ref[:hw, :])
               + _dot(ya_ref[rs, :].astype(BF16), w_ref[hw:, :]))
        x1 = x_ref[rs, :] + g1_ref[...] * acc
        o_ref[rs, :] = x1
        h_ref[rs, :] = (_rms(x1) * n2_ref[...] * (1.0 + sc_ref[...]) + sh_ref[...]).astype(BF16)


def _outproj(x2, y_hg, y_at, mod, mod_spec, norm2, w_out, tm):
    m, d = x2.shape
    return pl.pallas_call(
        _outproj_kernel,
        grid=(m // tm,),
        in_specs=[pl.BlockSpec((tm, d), lambda i: (i, 0)),
                  pl.BlockSpec((tm, y_hg.shape[1]), lambda i: (i, 0)),
                  pl.BlockSpec((tm, y_at.shape[1]), lambda i: (i, 0)),
                  mod_spec(2), mod_spec(3), mod_spec(4),
                  pl.BlockSpec((1, d), lambda i: (0, 0)),
                  pl.BlockSpec(w_out.shape, lambda i: (0, 0))],
        out_specs=[pl.BlockSpec((tm, d), lambda i: (i, 0)),
                   pl.BlockSpec((tm, d), lambda i: (i, 0))],
        out_shape=[jax.ShapeDtypeStruct((m, d), F32),
                   jax.ShapeDtypeStruct((m, d), BF16)],
        compiler_params=_params("arbitrary"),
        name="outproj",
    )(x2, y_hg, y_at, mod, mod, mod, norm2, w_out)


def _conv_gate(conv, u):
    return (_silu(conv) * u).astype(BF16)


def _ffn_up_kernel(h_ref, halo_ref, wa_ref, wu_ref, cw_ref, cb_ref, g_ref, tail_ref, h_scr, a_scr,
                   *, tm, tiles_per_seq, sub_blocks):
    i = pl.program_id(0)

    @pl.when(pl.program_id(1) == 0)
    def _():
        h_scr[:HALO, :] = halo_ref[...]
        h_scr[HALO:, :] = h_ref[...]

    rows = tm // sub_blocks
    for r in range(sub_blocks):
        lo = HALO + r * rows
        if r == 0:
            a_scr[:lo + rows, :] = _dot(h_scr[:lo + rows, :], wa_ref[...])

            @pl.when(i % tiles_per_seq == 0)
            def _():
                a_scr[:HALO, :] = jnp.zeros((HALO, a_scr.shape[1]), F32)
        else:
            a_scr[lo:lo + rows, :] = _dot(h_scr[lo:lo + rows, :], wa_ref[...])
        u = _dot(h_scr[lo:lo + rows, :], wu_ref[...])
        conv = (cb_ref[...] + a_scr[lo - 2:lo - 2 + rows, :] * cw_ref[0:1, :]
                + a_scr[lo - 1:lo - 1 + rows, :] * cw_ref[1:2, :]
                + a_scr[lo:lo + rows, :] * cw_ref[2:3, :])
        g_ref[r * rows:(r + 1) * rows, :] = _conv_gate(conv, u)
    tail_ref[...] = a_scr[HALO + tm - 2:, :]


def _ffn_up(h2, w_up, conv_w, conv_b, seq, tm=1024, tf=512, sub_blocks=4):
    m, d = h2.shape
    ff = w_up.shape[1] // 2
    nf = ff // tf
    tm = min(tm, seq)
    tps = seq // tm
    assert conv_w.shape[0] == 3 and ff % tf == 0 and seq % tm == 0
    g, tails = pl.pallas_call(
        functools.partial(_ffn_up_kernel, tm=tm, tiles_per_seq=tps, sub_blocks=sub_blocks),
        grid=(m // tm, nf),
        in_specs=[pl.BlockSpec((tm, d), lambda i, f: (i, 0)),
                  pl.BlockSpec((HALO, d), lambda i, f: (jnp.maximum(i * (tm // HALO) - 1, 0), 0)),
                  pl.BlockSpec((d, tf), lambda i, f: (0, f)),
                  pl.BlockSpec((d, tf), lambda i, f: (0, nf + f)),
                  pl.BlockSpec((3, tf), lambda i, f: (0, f)),
                  pl.BlockSpec((1, tf), lambda i, f: (0, f))],
        out_specs=[pl.BlockSpec((tm, tf), lambda i, f: (i, f)),
                   pl.BlockSpec((None, 2, tf), lambda i, f: (i, 0, f))],
        out_shape=[jax.ShapeDtypeStruct((m, ff), BF16),
                   jax.ShapeDtypeStruct((m // tm, 2, ff), F32)],
        scratch_shapes=[pltpu.VMEM((HALO + tm, d), BF16), pltpu.VMEM((HALO + tm, tf), F32)],
        compiler_params=_params("arbitrary", "arbitrary"),
        name="ffn_up",
    )(h2, h2, w_up, w_up, conv_w, conv_b)
    return g, tails[tps - 1::tps]


def _ffn_down_kernel(g_ref, x_ref, g2_ref, wd_ref, fn_ref, y_ref, ss_scr, *, tn, sub_blocks):
    n = pl.program_id(1)
    d = y_ref.shape[1]

    @pl.when(n == 0)
    def _():
        ss_scr[...] = jnp.zeros_like(ss_scr)

    rows = y_ref.shape[0] // sub_blocks
    for r in range(sub_blocks):
        rs = slice(r * rows, (r + 1) * rows)
        x2 = x_ref[rs, :] + g2_ref[...] * _dot(g_ref[rs, :], wd_ref[...])
        ss_scr[rs, :] += jnp.sum(x2 * x2, axis=-1, keepdims=True)
        y_ref[rs, pl.ds(pl.multiple_of(n * tn, tn), tn)] = x2

    @pl.when(n == d // tn - 1)
    def _():
        y_ref[...] = y_ref[...] * lax.rsqrt(ss_scr[...] / d + EPS) * fn_ref[...]


def _ffn_down(g, x1, mod_p, w_down, final_norm, seq, tm=1024, tn=256, sub_blocks=4):
    m, d = x1.shape
    ff = w_down.shape[0]
    tm = min(tm, seq)
    tps = seq // tm
    nd = d // tn
    assert d % tn == 0 and seq % tm == 0
    return pl.pallas_call(
        functools.partial(_ffn_down_kernel, tn=tn, sub_blocks=sub_blocks),
        grid=(m // tm, nd),
        in_specs=[pl.BlockSpec((tm, ff), lambda i, n: (i, 0)),
                  pl.BlockSpec((tm, tn), lambda i, n: (i, n)),
                  pl.BlockSpec((None, 1, tn), lambda i, n: (i // tps, 0, 5 * nd + n)),
                  pl.BlockSpec((ff, tn), lambda i, n: (0, n)),
                  pl.BlockSpec((1, d), lambda i, n: (0, 0))],
        out_specs=pl.BlockSpec((tm, d), lambda i, n: (i, 0)),
        out_shape=jax.ShapeDtypeStruct((m, d), F32),
        scratch_shapes=[pltpu.VMEM((tm, 1), F32)],
        compiler_params=_params("arbitrary", "arbitrary"),
        name="ffn_down",
    )(g, x1, mod_p, w_down, final_norm)


def _ffn_sample_kernel(x_ref, h_ref, g2_ref, p0_ref, p1_ref, wa_ref, wu_ref, cw_ref,
                       cb_ref, wd_ref, fn_ref, y_ref, a_ref, acc_scr):
    f = pl.program_id(0)

    @pl.when(f == 0)
    def _():
        acc_scr[...] = jnp.zeros_like(acc_scr)

    a = _dot(h_ref[...], wa_ref[...])
    u = _dot(h_ref[...], wu_ref[...])
    conv = (cb_ref[...] + p0_ref[...] * cw_ref[0:1, :] + p1_ref[...] * cw_ref[1:2, :]
            + a * cw_ref[2:3, :])
    acc_scr[...] += _dot(_conv_gate(conv, u), wd_ref[...])
    a_ref[...] = a

    @pl.when(f == pl.num_programs(0) - 1)
    def _():
        y_ref[...] = _rms(x_ref[...] + g2_ref[...] * acc_scr[...]) * fn_ref[...]


def _ffn_sample(x1, h2, mod_s, prev2, w_up, conv_w, conv_b, w_down, final_norm, tf=512):
    m, d = x1.shape
    ff = w_down.shape[0]
    nf = ff // tf

    return pl.pallas_call(
        _ffn_sample_kernel,
        grid=(nf,),
        in_specs=[pl.BlockSpec((m, d), lambda f: (0, 0)),
                  pl.BlockSpec((m, d), lambda f: (0, 0)),
                  pl.BlockSpec((m, d), lambda f: (0, 5)),
                  pl.BlockSpec((m, tf), lambda f: (0, f)),
                  pl.BlockSpec((m, tf), lambda f: (0, nf + f)),
                  pl.BlockSpec((d, tf), lambda f: (0, f)),
                  pl.BlockSpec((d, tf), lambda f: (0, nf + f)),
                  pl.BlockSpec((3, tf), lambda f: (0, f)),
                  pl.BlockSpec((1, tf), lambda f: (0, f)),
                  pl.BlockSpec((tf, d), lambda f: (f, 0)),
                  pl.BlockSpec((1, d), lambda f: (0, 0))],
        out_specs=[pl.BlockSpec((m, d), lambda f: (0, 0)),
                   pl.BlockSpec((m, tf), lambda f: (0, f))],
        out_shape=[jax.ShapeDtypeStruct((m, d), F32),
                   jax.ShapeDtypeStruct((m, ff), F32)],
        scratch_shapes=[pltpu.VMEM((m, d), F32)],
        compiler_params=_params("arbitrary"),
        name="ffn_sample",
    )(x1, h2, mod_s, prev2, prev2, w_up, w_up, conv_w, conv_b, w_down, final_norm)


def _rope_tables(pos, qk_dim):
    rot = qk_dim // 4
    half = rot // 2
    inv_freq = ROPE_THETA ** (-jnp.arange(half, dtype=F32) / half)
    ang = pos.astype(F32)[:, None] * inv_freq[None, :]
    cos, sin = jnp.cos(ang), jnp.sin(ang)
    ones = jnp.ones((pos.shape[0], qk_dim - rot), F32)
    zeros = jnp.zeros((pos.shape[0], qk_dim - rot), F32)
    zh = jnp.zeros_like(sin)
    cos_t = jnp.concatenate([cos, cos, ones] * 2, axis=1)
    sin_a = jnp.concatenate([zh, sin, zeros] * 2, axis=1)
    sin_b = jnp.concatenate([-sin, zh, zeros] * 2, axis=1)
    return cos_t, sin_a, sin_b


def kernel(x_prompt, x_sample, c_prompt, c_sample, cache_k, cache_v, state_hgrn, state_conv, page_table,
           w_ada, b_ada, norm1, norm2, w_in, hg_lb_logits, hg_norm, lam_q1, lam_k1, lam_q2, lam_k2,
           at_norm, w_out, w_up, conv_w, conv_b, w_down, final_norm):
    n_batch, seq, d = x_prompt.shape
    n_seq = x_sample.shape[0]
    depth = w_in.shape[0]
    assert depth == 1 and x_sample.shape[1] == 1
    n_heads = state_hgrn.shape[2]
    assert cache_k.shape[3] == n_heads and cache_k.shape[4] == LANES and state_hgrn.shape[3] == LANES
    qk_dim = lam_q1.shape[1]
    past_len = page_table.shape[1] * cache_k.shape[2]
    lam_init = 0.8 - 0.6 * math.exp(-0.3 * 0)
    ff = w_down.shape[1]
    width = n_heads * LANES
    lam_params = [lam_q1, lam_k1, lam_q2, lam_k2]

    w_in16 = w_in[0].astype(BF16)
    w_out16 = w_out[0].astype(BF16)
    w_up16 = w_up[0].astype(BF16)
    w_down16 = w_down[0].astype(BF16)

    mod = _adaln(jnp.concatenate([c_prompt, c_sample], axis=0), w_ada[0], b_ada[0])
    mod_p = mod[:n_batch].reshape(n_batch, 1, N_MOD * d)
    mod_s = mod[n_batch:]

    tm_in = min(1024, seq)
    tps_in = seq // tm_in
    tm_out = min(512, seq)
    tps_out = seq // tm_out
    xp = x_prompt.reshape(n_batch * seq, d)
    tables_p = _rope_tables(jnp.arange(seq, dtype=jnp.int32), qk_dim)

    def pspec(part):
        return pl.BlockSpec((None, 1, d), lambda i, j: (i // tps_in, 0, part))

    zq, k_p, v_p = _inproj(xp, mod_p, mod_p, (pspec(0), pspec(1)), norm1, w_in16, tables_p,
                           pl.BlockSpec((tm_in, LANES), lambda i, j: (i % tps_in, 0)), tm_in, n_heads)

    xs = x_sample.reshape(n_seq, d)
    tables_s = _rope_tables(jnp.full((n_seq,), past_len, jnp.int32), qk_dim)

    def sspec(part):
        return pl.BlockSpec((n_seq, d), lambda i, j: (0, part))

    zq_s, k_s, v_s = _inproj(xs, mod_s, mod_s, (sspec(0), sspec(1)), norm1, w_in16, tables_s,
                             pl.BlockSpec((n_seq, LANES), lambda i, j: (0, 0)), n_seq, n_heads, split=1)
    zq_s4 = zq_s.reshape(5, n_seq, n_heads, LANES)

    y_at_s, y_hg, hgrn_p = _pages_and_hgrn(
        page_table, lam_params, at_norm, zq_s4[4], k_s.reshape(n_seq, n_heads, LANES),
        v_s.reshape(n_seq, n_heads, LANES), cache_k, cache_v, lam_init,
        zq, hg_lb_logits, hg_norm, n_batch, seq)

    y_at = _attn_prompt(lam_params, at_norm, zq, k_p, v_p, n_batch, seq, n_heads, lam_init)
    x1, h2 = _outproj(xp, y_hg, y_at, mod_p,
                      lambda part: pl.BlockSpec((None, 1, d), lambda i: (i // tps_out, 0, part)),
                      norm2, w_out16, tm_out)
    g_p, conv_p = _ffn_up(h2, w_up16, conv_w[0], conv_b, seq)
    y_p = _ffn_down(g_p, x1, mod_p, w_down16, final_norm.reshape(1, d), seq)

    y_hg_s, hgrn_s = _hgrn_sample(zq_s4, hg_lb_logits.reshape(-1, n_heads, LANES), hg_norm, state_hgrn[0])
    x1_s, h2_s = _outproj(xs, y_hg_s.reshape(n_seq, width), y_at_s.reshape(n_seq, width), mod_s,
                          lambda part: pl.BlockSpec((n_seq, d), lambda i: (0, part)),
                          norm2, w_out16, n_seq)
    prev = state_conv[0]
    y_s, a_s = _ffn_sample(x1_s, h2_s, mod_s, prev.reshape(n_seq, 2 * ff), w_up16, conv_w[0], conv_b,
                           w_down16, final_norm.reshape(1, d))
    conv_s = jnp.stack([prev[:, 1, :], a_s], axis=1)

    return (y_p.reshape(n_batch, seq, d), y_s.reshape(n_seq, 1, d),
            k_p.reshape(1, n_batch, seq, n_heads, LANES), v_p.reshape(1, n_batch, seq, n_heads, LANES),
            k_s.reshape(1, n_seq, 1, n_heads, LANES), v_s.reshape(1, n_seq, 1, n_heads, LANES),
            hgrn_p[None], hgrn_s[None], conv_p[None], conv_s[None])
```

```python
import functools
import math

import jax
import jax.numpy as jnp
from jax import lax
from jax.experimental import pallas as pl
from jax.experimental.pallas import tpu as pltpu

EPS = 1e-6
ROPE_THETA = 500000.0
LOG2_E = math.log2(math.e)
N_MOD = 6
LANES = 128
HGRN_CHUNK = 64
HGRN_SUB = 16
HGRN_TILE = 8
HALO = 16
VMEM_LIMIT_BYTES = 56 * 1024 * 1024

F32 = jnp.float32
BF16 = jnp.bfloat16


def _params(*sem):
    return pltpu.CompilerParams(dimension_semantics=sem, vmem_limit_bytes=VMEM_LIMIT_BYTES)


def _silu(x):
    return x / (1.0 + jnp.exp(-x))


def _sigmoid(x):
    return 1.0 / (1.0 + jnp.exp(-x))


def _rms(x):
    return x * lax.rsqrt(jnp.mean(x * x, axis=-1, keepdims=True) + EPS)


def _dot(a, b):
    return jnp.dot(a, b, preferred_element_type=F32)


def _dot_nt(a, b):
    return lax.dot_general(a, b, (((1,), (1,)), ((), ())), preferred_element_type=F32)


def _lam_value(lq1, lk1, lq2, lk2, lam_init):
    return (jnp.exp(jnp.sum(lq1 * lk1, axis=-1, keepdims=True))
            - jnp.exp(jnp.sum(lq2 * lk2, axis=-1, keepdims=True)) + lam_init)


def _lower_bound(layer_logits):
    mx = functools.reduce(jnp.maximum, layer_logits)
    es = [jnp.exp(r - mx) for r in layer_logits]
    return es[0] / functools.reduce(jnp.add, es)


def _adaln_kernel(c_ref, w_ref, b_ref, o_ref):
    s = _silu(c_ref[...]).astype(BF16)
    o_ref[...] = _dot(s, w_ref[...].astype(BF16)) + b_ref[...]


def _adaln(c_all, w_ada, b_ada, tn=512):
    rows, d = c_all.shape
    n = w_ada.shape[1]
    assert n % tn == 0
    return pl.pallas_call(
        _adaln_kernel,
        grid=(n // tn,),
        in_specs=[pl.BlockSpec((rows, d), lambda j: (0, 0)),
                  pl.BlockSpec((d, tn), lambda j: (0, j)),
                  pl.BlockSpec((1, tn), lambda j: (0, j))],
        out_specs=pl.BlockSpec((rows, tn), lambda j: (0, j)),
        out_shape=jax.ShapeDtypeStruct((rows, n), F32),
        compiler_params=_params("arbitrary"),
        name="adaln",
    )(c_all, w_ada, b_ada.reshape(1, n))


def _rope(blk, cos_t, sin_a, sin_b):
    return (blk * cos_t + pltpu.roll(blk, 8, axis=1) * sin_a
            + pltpu.roll(blk, LANES - 8, axis=1) * sin_b)


def _inproj_kernel(x_ref, sh_ref, sc_ref, g_ref, w_ref, cos_ref, sa_ref, sb_ref,
                   zq_ref, k_ref, v_ref, h_scr, *, split, sub_blocks):
    j = pl.program_id(1)
    part = j // split

    tm, tn = zq_ref.shape
    rows = tm // sub_blocks

    def rows_of(ref, rs):
        return ref[...] if ref.shape[0] == 1 else ref[rs, :]

    def project(dst, rotary=False, modulate=False):
        for r in range(sub_blocks):
            rs = slice(r * rows, (r + 1) * rows)
            if modulate:
                h = _rms(x_ref[rs, :]) * g_ref[...] * (1.0 + rows_of(sc_ref, rs)) + rows_of(sh_ref, rs)
                h_scr[rs, :] = h.astype(BF16)
            acc = _dot(h_scr[rs, :], w_ref[...])
            if rotary:
                for hh in range(tn // LANES):
                    sl = slice(hh * LANES, (hh + 1) * LANES)
                    dst[rs, sl] = _rope(acc[:, sl], cos_ref[rs, :], sa_ref[rs, :], sb_ref[rs, :])
            else:
                dst[rs, :] = acc

    @pl.when(j == 0)
    def _():
        project(zq_ref, modulate=True)

    @pl.when((j > 0) & (part < 4))
    def _():
        project(zq_ref)

    @pl.when(part == 4)
    def _():
        project(zq_ref, rotary=True)

    @pl.when(part == 5)
    def _():
        project(k_ref, rotary=True)

    @pl.when(part == 6)
    def _():
        project(v_ref)


def _inproj(x2, sh, sc, mod_spec, norm_g, w_in, tables, tab_spec, tm, n_heads, split=2):
    m, d = x2.shape
    width = n_heads * LANES
    assert w_in.shape[1] == 7 * width and n_heads % split == 0
    tn = width // split
    sh_spec, sc_spec = mod_spec

    def zq_map(i, j):
        jz = jnp.minimum(j, 5 * split - 1)
        return (jz // split, i, jz % split)

    return pl.pallas_call(
        functools.partial(_inproj_kernel, split=split, sub_blocks=max(1, tm // 256)),
        grid=(m // tm, 7 * split),
        in_specs=[pl.BlockSpec((tm, d), lambda i, j: (i, 0)),
                  sh_spec, sc_spec,
                  pl.BlockSpec((1, d), lambda i, j: (0, 0)),
                  pl.BlockSpec((d, tn), lambda i, j: (0, j)),
                  tab_spec, tab_spec, tab_spec],
        out_specs=[pl.BlockSpec((None, tm, tn), zq_map),
                   pl.BlockSpec((tm, tn), lambda i, j: (i, jnp.clip(j - 5 * split, 0, split - 1))),
                   pl.BlockSpec((tm, tn), lambda i, j: (i, jnp.clip(j - 6 * split, 0, split - 1)))],
        out_shape=[jax.ShapeDtypeStruct((5, m, width), F32),
                   jax.ShapeDtypeStruct((m, width), F32),
                   jax.ShapeDtypeStruct((m, width), F32)],
        scratch_shapes=[pltpu.VMEM((tm, d), BF16)],
        compiler_params=_params("arbitrary", "arbitrary"),
        name="inproj",
    )(x2, sh, sc, norm_g, w_in, *tables)


def _inproj_resident_kernel(x_ref, sh_ref, sc_ref, g_ref, w_ref, cos_ref, sa_ref, sb_ref,
                            zq_ref, k_ref, v_ref):
    width = k_ref.shape[1]
    h = (_rms(x_ref[...]) * g_ref[...] * (1.0 + sc_ref[...]) + sh_ref[...]).astype(BF16)
    for part in range(7):
        acc = _dot(h, w_ref[:, part * width:(part + 1) * width])
        if part in (4, 5):
            dst = zq_ref.at[4] if part == 4 else k_ref
            for hh in range(width // LANES):
                sl = slice(hh * LANES, (hh + 1) * LANES)
                dst[:, sl] = _rope(acc[:, sl], cos_ref[...], sa_ref[...], sb_ref[...])
        elif part == 6:
            v_ref[...] = acc
        else:
            zq_ref[part] = acc


def _inproj_resident(x2, sh, sc, mod_spec, norm_g, w_in, tables, tab_spec, tm, n_heads):
    m, d = x2.shape
    width = n_heads * LANES
    assert w_in.shape[1] == 7 * width
    sh_spec, sc_spec = mod_spec
    return pl.pallas_call(
        _inproj_resident_kernel,
        grid=(m // tm,),
        in_specs=[pl.BlockSpec((tm, d), lambda i: (i, 0)),
                  sh_spec, sc_spec,
                  pl.BlockSpec((1, d), lambda i: (0, 0)),
                  pl.BlockSpec(w_in.shape, lambda i: (0, 0), pipeline_mode=pl.Buffered(1)),
                  tab_spec, tab_spec, tab_spec],
        out_specs=[pl.BlockSpec((5, tm, width), lambda i: (0, i, 0)),
                   pl.BlockSpec((tm, width), lambda i: (i, 0)),
                   pl.BlockSpec((tm, width), lambda i: (i, 0))],
        out_shape=[jax.ShapeDtypeStruct((5, m, width), F32),
                   jax.ShapeDtypeStruct((m, width), F32),
                   jax.ShapeDtypeStruct((m, width), F32)],
        compiler_params=_params("arbitrary"),
        name="inproj",
    )(x2, sh, sc, norm_g, w_in, *tables)


def _roll_in_tiles(x, shift):
    if shift == 0:
        return x
    tiles = [pltpu.roll(x[r:r + HGRN_TILE, :], shift, axis=0) for r in range(0, x.shape[0], HGRN_TILE)]
    return jnp.concatenate(tiles, axis=0)


def _hgrn_masks():
    c_rows = HGRN_CHUNK
    ri = lax.broadcasted_iota(jnp.int32, (c_rows, c_rows), 0)
    ci = lax.broadcasted_iota(jnp.int32, (c_rows, c_rows), 1)
    tri = jnp.where(ri >= ci, 1.0, 0.0).astype(BF16)
    same_sub = ri // HGRN_SUB == ci // HGRN_SUB
    same_tile = ri // HGRN_TILE == ci // HGRN_TILE
    diag = [same_tile & (ci == ri - dlt) for dlt in range(HGRN_TILE)]
    row = lax.broadcasted_iota(jnp.int32, (c_rows, 1), 0)
    return tri, same_sub, diag, row


def _hgrn_chunk(q, hf, v, g, lb, gn, st, masks):
    c_rows, sub, tile = HGRN_CHUNK, HGRN_SUB, HGRN_TILE
    tri, same_sub, diag, row = masks
    f = lb + (1.0 - lb) * _sigmoid(hf)
    kk = 1.0 - f
    lf = jnp.log(f)
    hi = lf.astype(BF16)
    r1 = lf - hi.astype(F32)
    mid = r1.astype(BF16)
    lo = (r1 - mid.astype(F32)).astype(BF16)
    b = _dot(tri, hi) + _dot(tri, mid) + _dot(tri, lo)
    b_last = b[c_rows - 1:c_rows, :]

    o = _dot_nt((q * jnp.exp(b)).astype(BF16), st.astype(BF16))
    kd = kk * jnp.exp(b_last - b)
    st_new = st * jnp.exp(b_last) + _dot(v.T.astype(BF16), kd.astype(BF16))

    v16 = v.astype(BF16)
    for jb in range(c_rows // sub - 1):
        e = (jb + 1) * sub
        r_ref = b[e - 1:e, :]
        qj = q * jnp.exp(jnp.minimum(b - r_ref, 0.0))
        kj = kk[jb * sub:e, :] * jnp.exp(r_ref - b[jb * sub:e, :])
        att = _dot_nt(qj.astype(BF16), kj.astype(BF16))
        att = jnp.where(row >= e, att, 0.0)
        o = o + _dot(att.astype(BF16), v16[jb * sub:e, :])
    zero_tile = jnp.zeros((tile, LANES), F32)
    q_parts, k_parts = [], []
    for r0 in range(0, c_rows, sub):
        r_ref = b[r0 + tile - 1:r0 + tile, :]
        first, second = slice(r0, r0 + tile), slice(r0 + tile, r0 + sub)
        k_parts += [kk[first, :] * jnp.exp(r_ref - b[first, :]), zero_tile]
        q_parts += [zero_tile, q[second, :] * jnp.exp(b[second, :] - r_ref)]
    att = _dot_nt(jnp.concatenate(q_parts, axis=0).astype(BF16), jnp.concatenate(k_parts, axis=0).astype(BF16))
    att = jnp.where(same_sub, att, 0.0)
    decay = None
    for dlt in range(tile):
        if dlt == 0:
            w = q * kk
        else:
            gate = _roll_in_tiles(f, dlt - 1)
            decay = gate if decay is None else decay * gate
            w = q * _roll_in_tiles(kk, dlt) * decay
        att = att + jnp.where(diag[dlt], jnp.sum(w, axis=-1, keepdims=True), 0.0)
    o = o + _dot(att.astype(BF16), v16)
    return _rms(o) * gn * _silu(g), st_new


def _hgrn_prompt_step(q_ref, f_ref, v_ref, g_ref, lbl_ref, gn_ref, y_ref, st_scr, group):
    c_rows = HGRN_CHUNK
    lb = _lower_bound([lbl_ref[l:l + 1, :] for l in range(lbl_ref.shape[0])])
    masks = _hgrn_masks()
    for r0 in range(0, q_ref.shape[0], c_rows):
        rs = slice(r0, r0 + c_rows)
        for gi in range(group):
            sl = slice(gi * LANES, (gi + 1) * LANES)
            y, st_new = _hgrn_chunk(q_ref[rs, sl], f_ref[rs, sl], v_ref[rs, sl], g_ref[rs, sl],
                                    lb[:, sl], gn_ref[...], st_scr[gi], masks)
            st_scr[gi] = st_new
            y_ref[rs, sl] = y


def _hgrn_sample_kernel(q_ref, f_ref, v_ref, g_ref, lbl_ref, gn_ref, s_ref, y_ref, so_ref,
                        *, n_heads):
    lb = _lower_bound([lbl_ref[l] for l in range(lbl_ref.shape[0])])
    pad = jnp.zeros((LANES - 3 * n_heads, LANES), F32)
    for s in range(q_ref.shape[0]):
        q = q_ref[s]
        v = v_ref[s]
        f = lb + (1.0 - lb) * _sigmoid(f_ref[s])
        kk = 1.0 - f
        cols = jnp.concatenate([f, kk, q, pad], axis=0).T
        o_rows = []
        for h in range(n_heads):
            f_col = cols[:, h:h + 1]
            k_col = cols[:, n_heads + h:n_heads + h + 1]
            q_col = cols[:, 2 * n_heads + h:2 * n_heads + h + 1]
            s_new = f_col * s_ref[s, h] + k_col * v[h:h + 1, :]
            so_ref[s, h] = s_new
            o_rows.append(jnp.sum(q_col * s_new, axis=0, keepdims=True))
        o = jnp.concatenate(o_rows, axis=0)
        y_ref[s] = _rms(o) * gn_ref[...] * _silu(g_ref[s])


def _hgrn_sample(zq4, lb_logits3, hg_norm, state, per_step=4):
    n_seq, n_heads = state.shape[0], state.shape[1]
    per_step = math.gcd(per_step, n_seq)

    def zspec(part):
        return pl.BlockSpec((None, per_step, n_heads, LANES), lambda b: (part, b, 0, 0))

    return pl.pallas_call(
        functools.partial(_hgrn_sample_kernel, n_heads=n_heads),
        grid=(n_seq // per_step,),
        in_specs=[zspec(0), zspec(1), zspec(2), zspec(3),
                  pl.BlockSpec(lb_logits3.shape, lambda b: (0, 0, 0)),
                  pl.BlockSpec((1, LANES), lambda b: (0, 0)),
                  pl.BlockSpec((per_step, n_heads, LANES, LANES), lambda b: (b, 0, 0, 0))],
        out_specs=[pl.BlockSpec((per_step, n_heads, LANES), lambda b: (b, 0, 0)),
                   pl.BlockSpec((per_step, n_heads, LANES, LANES), lambda b: (b, 0, 0, 0))],
        out_shape=[jax.ShapeDtypeStruct((n_seq, n_heads, LANES), F32),
                   jax.ShapeDtypeStruct(state.shape, F32)],
        compiler_params=_params("arbitrary"),
        name="hgrn_sample",
    )(zq4, zq4, zq4, zq4, lb_logits3, hg_norm, state)


def _attn_prompt_kernel(lq1_ref, lk1_ref, lq2_ref, lk2_ref, an_ref, q_ref, k_ref, v_ref, y_ref,
                        k16_scr, vt_scr, qm_scr, sa_scr, sb_scr, m_scr, l_scr, acc_scr, *, lam_init, tile,
                        qk_dim):
    n_tiles = vt_scr.shape[0]
    tq = tk = tile

    k16_scr[...] = k_ref[...].astype(BF16)
    for t in range(n_tiles):
        vt_scr[t] = v_ref[t * tk:(t + 1) * tk, :].T.astype(BF16)
    lam = _lam_value(lq1_ref[...], lk1_ref[...], lq2_ref[...], lk2_ref[...], lam_init)
    lane = lax.broadcasted_iota(jnp.int32, (1, LANES), 1)
    key = lax.broadcasted_iota(jnp.int32, (tk, 2 * tq), 0)
    qry = lax.broadcasted_iota(jnp.int32, (tk, 2 * tq), 1) % tq
    causal = key <= qry
    score_bufs = (sa_scr, sb_scr)

    for qi in range(n_tiles):
        par = qi % 2
        qm, m_s, l_s, acc = qm_scr.at[par], m_scr.at[par], l_scr.at[par], acc_scr.at[par]
        q = q_ref[qi * tq:(qi + 1) * tq, :] * (qk_dim ** -0.5 * LOG2_E)
        qm[:tq, :] = jnp.where(lane < qk_dim, q, 0.0).astype(BF16)
        qm[tq:, :] = jnp.where(lane >= qk_dim, q, 0.0).astype(BF16)
        m_s[...] = jnp.full(m_s.shape, -jnp.inf, F32)
        l_s[...] = jnp.zeros(l_s.shape, F32)
        acc[...] = jnp.zeros(acc.shape, F32)

        def scores(kj, dst, qm=qm):
            dst[...] = _dot_nt(k16_scr[kj * tk:(kj + 1) * tk, :], qm[...])

        def consume(kj, src, on_diagonal, m_s=m_s, l_s=l_s, acc=acc):
            for c in range(2):
                cs = slice(c * tq, (c + 1) * tq)
                st = src[:, cs]
                if on_diagonal:
                    st = jnp.where(causal[:, cs], st, -jnp.inf)
                m_old = m_s[:, cs]
                m_new = jnp.maximum(m_old, jnp.max(st, axis=0, keepdims=True))
                p = jnp.exp2(st - m_new)
                corr = jnp.exp2(m_old - m_new)
                l_s[:, cs] = l_s[:, cs] * corr + jnp.sum(p.reshape(tk // 8, 8, tq), axis=0)
                acc[:, cs] = acc[:, cs] * corr + _dot(vt_scr[kj], p.astype(BF16))
                m_s[:, cs] = m_new

        scores(0, score_bufs[0])
        for kj in range(qi + 1):
            if kj < qi:
                scores(kj + 1, score_bufs[(kj + 1) % 2])
            consume(kj, score_bufs[kj % 2], kj == qi)

        o = acc[...] / jnp.sum(l_s[...], axis=0, keepdims=True)
        o = (o[:, :tq] - lam * o[:, tq:]).T
        y_ref[qi * tq:(qi + 1) * tq, :] = _rms(o) * an_ref[...] * (1.0 - lam_init)


def _attn_prompt(lam_params, at_norm, zq, k, v, n_batch, seq, n_heads, lam_init, tile=512):
    m = k.shape[0]
    nt = seq // tile
    qk_dim = lam_params[0].shape[1]
    small = [pl.BlockSpec(p.shape, lambda b, h: (0, 0)) for p in lam_params]
    seq_spec = pl.BlockSpec((seq, LANES), lambda b, h: (b, h))

    return pl.pallas_call(
        functools.partial(_attn_prompt_kernel, lam_init=lam_init, tile=tile, qk_dim=qk_dim),
        grid=(n_batch, n_heads),
        in_specs=small + [pl.BlockSpec((1, LANES), lambda b, h: (0, 0)),
                          pl.BlockSpec((None, seq, LANES), lambda b, h: (4, b, h)),
                          seq_spec, seq_spec],
        out_specs=seq_spec,
        out_shape=jax.ShapeDtypeStruct((m, n_heads * LANES), F32),
        scratch_shapes=[pltpu.VMEM((seq, LANES), BF16), pltpu.VMEM((nt, LANES, tile), BF16),
                        pltpu.VMEM((2, 2 * tile, LANES), BF16),
                        pltpu.VMEM((tile, 2 * tile), F32), pltpu.VMEM((tile, 2 * tile), F32),
                        pltpu.VMEM((2, 1, 2 * tile), F32),
                        pltpu.VMEM((2, 8, 2 * tile), F32), pltpu.VMEM((2, LANES, 2 * tile), F32)],
        compiler_params=_params("arbitrary", "arbitrary"),
        name="attn_prompt",
    )(*lam_params, at_norm, zq, k, v)


def _pages_hgrn_kernel(pt_ref, lq1_ref, lk1_ref, lq2_ref, lk2_ref, an_ref, q_ref, kn_ref, vn_ref,
                       z_ref, lbl_ref, gn_ref, *rest,
                       lam_init, n_heads, qk_dim, pages, group, sweep_steps):
    k_refs = rest[:pages]
    v_refs = rest[pages:2 * pages]
    y_ref, yh_ref, sfin_ref, m_scr, l_scr, acc_scr, st_scr = rest[2 * pages:]
    j = pl.program_id(1)
    page_rows = k_refs[0].shape[0] * n_heads
    sweep_pos = (pl.program_id(0) * pl.num_programs(1) + j) % sweep_steps

    @pl.when(sweep_pos == 0)
    def _():
        st_scr[...] = jnp.zeros_like(st_scr)

    q = q_ref[...] * (qk_dim ** -0.5)
    lane = lax.broadcasted_iota(jnp.int32, (1, LANES), 1)
    qm = jnp.concatenate([jnp.where(lane < qk_dim, q, 0.0), jnp.where(lane >= qk_dim, q, 0.0)], axis=0)

    @pl.when(j == 0)
    def _():
        kn = kn_ref[...]
        vn = vn_ref[...]
        m_scr[...] = jnp.sum(qm * jnp.concatenate([kn, kn], axis=0), axis=-1, keepdims=True)
        l_scr[...] = jnp.ones_like(l_scr)
        acc_scr[...] = jnp.concatenate([vn, vn], axis=0)

    qm16 = qm.astype(BF16)
    own_head = (lax.broadcasted_iota(jnp.int32, (2 * n_heads, page_rows), 1) % n_heads
                == lax.broadcasted_iota(jnp.int32, (2 * n_heads, page_rows), 0) % n_heads)
    _hgrn_prompt_step(z_ref.at[0], z_ref.at[1], z_ref.at[2], z_ref.at[3], lbl_ref, gn_ref, yh_ref, st_scr,
                      group)

    scores = [jnp.where(own_head, _dot_nt(qm16, k_refs[p][...].reshape(page_rows, LANES).astype(BF16)),
                        -jnp.inf) for p in range(pages)]
    m_old = m_scr[...]
    m_new = functools.reduce(jnp.maximum, [jnp.max(s, axis=-1, keepdims=True) for s in scores] + [m_old])
    corr = jnp.exp(m_old - m_new)
    l_new = l_scr[...] * corr
    acc = acc_scr[...] * corr
    for p in range(pages):
        pr = jnp.exp(scores[p] - m_new)
        l_new = l_new + jnp.sum(pr, axis=-1, keepdims=True)
        acc = acc + _dot(pr.astype(BF16), v_refs[p][...].reshape(page_rows, LANES).astype(BF16))
    l_scr[...] = l_new
    acc_scr[...] = acc
    m_scr[...] = m_new

    @pl.when(sweep_pos == sweep_steps - 1)
    def _():
        for gi in range(group):
            sfin_ref[gi] = st_scr[gi].T

    @pl.when(j == pl.num_programs(1) - 1)
    def _():
        lam = _lam_value(lq1_ref[...], lk1_ref[...], lq2_ref[...], lk2_ref[...], lam_init)
        o = acc_scr[...] / l_scr[...]
        w = o[:n_heads] - lam * o[n_heads:]
        y_ref[...] = _rms(w) * an_ref[...] * (1.0 - lam_init)


def _pages_and_hgrn(page_table, lam_params, at_norm, q3, kn3, vn3, cache_k, cache_v, lam_init,
                    zq, lb_logits, hg_norm, n_batch, seq, pages=16, group=8):
    n_seq, n_pages = page_table.shape
    _, _, page_size, n_heads, _ = cache_k.shape
    qk_dim = lam_params[0].shape[1]
    assert n_pages % pages == 0
    page_steps = n_pages // pages
    n_steps = n_seq * page_steps
    group = math.gcd(group, n_heads)
    n_groups = n_heads // group
    gw = group * LANES
    m = zq.shape[1]
    assert (n_batch * n_groups * seq) % (n_steps * HGRN_CHUNK) == 0
    rows = n_batch * n_groups * seq // n_steps
    assert seq % rows == 0
    sweep_steps = seq // rows

    small = [pl.BlockSpec(p.shape, lambda b, j, pt: (0, 0)) for p in lam_params]
    row_spec = pl.BlockSpec((None, n_heads, LANES), lambda b, j, pt: (b, 0, 0))

    def page_spec(p):
        return pl.BlockSpec(
            (None, None, page_size, n_heads, LANES),
            lambda b, j, pt: (0, pt[b * n_pages + j * pages + p], 0, 0, 0))

    def sweep(b, j):
        step = b * page_steps + j
        return step // (sweep_steps * n_groups), (step // sweep_steps) % n_groups, step % sweep_steps

    def rows_map(b, j, pt):
        sb, sg, sp = sweep(b, j)
        return (sb * sweep_steps + sp, sg)

    z_spec = pl.BlockSpec((4, rows, gw), lambda b, j, pt: (0,) + rows_map(b, j, pt))

    grid_spec = pltpu.PrefetchScalarGridSpec(
        num_scalar_prefetch=1,
        grid=(n_seq, page_steps),
        in_specs=(small + [pl.BlockSpec((1, LANES), lambda b, j, pt: (0, 0)), row_spec, row_spec, row_spec]
                  + [z_spec,
                     pl.BlockSpec((lb_logits.shape[0], gw), lambda b, j, pt: (0, sweep(b, j)[1])),
                     pl.BlockSpec((1, LANES), lambda b, j, pt: (0, 0))]
                  + [page_spec(p) for p in range(pages)] + [page_spec(p) for p in range(pages)]),
        out_specs=[row_spec,
                   pl.BlockSpec((rows, gw), rows_map),
                   pl.BlockSpec((None, group, LANES, LANES),
                                lambda b, j, pt: (sweep(b, j)[0], sweep(b, j)[1], 0, 0))],
        scratch_shapes=[pltpu.VMEM((2 * n_heads, 1), F32), pltpu.VMEM((2 * n_heads, 1), F32),
                        pltpu.VMEM((2 * n_heads, LANES), F32), pltpu.VMEM((group, LANES, LANES), F32)],
    )
    return pl.pallas_call(
        functools.partial(_pages_hgrn_kernel, lam_init=lam_init, n_heads=n_heads, qk_dim=qk_dim,
                          pages=pages, group=group, sweep_steps=sweep_steps),
        grid_spec=grid_spec,
        out_shape=[jax.ShapeDtypeStruct((n_seq, n_heads, LANES), F32),
                   jax.ShapeDtypeStruct((m, n_heads * LANES), F32),
                   jax.ShapeDtypeStruct((n_batch, n_heads, LANES, LANES), F32)],
        compiler_params=_params("arbitrary", "arbitrary"),
        name="pages_hgrn",
    )(page_table.reshape(-1), *lam_params, at_norm, q3, kn3, vn3, zq, lb_logits, hg_norm,
      *([cache_k] * pages), *([cache_v] * pages))


def _outproj_kernel(x_ref, yh_ref, ya_ref, g1_ref, sh_ref, sc_ref, n2_ref, w_ref, o_ref, h_ref):
    hw = yh_ref.shape[1]
    rows = min(256, x_ref.shape[0])
    for r in range(x_ref.shape[0] // rows):
        rs = slice(r * rows, (r + 1) * rows)
        acc = (_dot(yh_ref[rs, :].astype(BF16), w_ref[:hw, :])
               + _dot(ya_ref[rs, :].astype(BF16), w_ref[hw:, :]))
        x1 = x_ref[rs, :] + g1_ref[...] * acc
        o_ref[rs, :] = x1
        h_ref[rs, :] = (_rms(x1) * n2_ref[...] * (1.0 + sc_ref[...]) + sh_ref[...]).astype(BF16)


def _outproj(x2, y_hg, y_at, mod, mod_spec, norm2, w_out, tm):
    m, d = x2.shape
    return pl.pallas_call(
        _outproj_kernel,
        grid=(m // tm,),
        in_specs=[pl.BlockSpec((tm, d), lambda i: (i, 0)),
                  pl.BlockSpec((tm, y_hg.shape[1]), lambda i: (i, 0)),
                  pl.BlockSpec((tm, y_at.shape[1]), lambda i: (i, 0)),
                  mod_spec(2), mod_spec(3), mod_spec(4),
                  pl.BlockSpec((1, d), lambda i: (0, 0)),
                  pl.BlockSpec(w_out.shape, lambda i: (0, 0))],
        out_specs=[pl.BlockSpec((tm, d), lambda i: (i, 0)),
                   pl.BlockSpec((tm, d), lambda i: (i, 0))],
        out_shape=[jax.ShapeDtypeStruct((m, d), F32),
                   jax.ShapeDtypeStruct((m, d), BF16)],
        compiler_params=_params("arbitrary"),
        name="outproj",
    )(x2, y_hg, y_at, mod, mod, mod, norm2, w_out)


def _conv_gate(conv, u):
    return (_silu(conv) * u).astype(BF16)


def _ffn_up_kernel(h_ref, halo_ref, wa_ref, wu_ref, cw_ref, cb_ref, g_ref, tail_ref, h_scr, a_scr,
                   *, tm, tiles_per_seq, sub_blocks):
    i = pl.program_id(0)

    @pl.when(pl.program_id(1) == 0)
    def _():
        h_scr[:HALO, :] = halo_ref[...]
        h_scr[HALO:, :] = h_ref[...]

    rows = tm // sub_blocks
    for r in range(sub_blocks):
        lo = HALO + r * rows
        if r == 0:
            a_scr[:lo + rows, :] = _dot(h_scr[:lo + rows, :], wa_ref[...])

            @pl.when(i % tiles_per_seq == 0)
            def _():
                a_scr[:HALO, :] = jnp.zeros((HALO, a_scr.shape[1]), F32)
        else:
            a_scr[lo:lo + rows, :] = _dot(h_scr[lo:lo + rows, :], wa_ref[...])
        u = _dot(h_scr[lo:lo + rows, :], wu_ref[...])
        conv = (cb_ref[...] + a_scr[lo - 2:lo - 2 + rows, :] * cw_ref[0:1, :]
                + a_scr[lo - 1:lo - 1 + rows, :] * cw_ref[1:2, :]
                + a_scr[lo:lo + rows, :] * cw_ref[2:3, :])
        g_ref[r * rows:(r + 1) * rows, :] = _conv_gate(conv, u)
    tail_ref[...] = a_scr[HALO + tm - 2:, :]


def _ffn_up(h2, w_up, conv_w, conv_b, seq, tm=1024, tf=512, sub_blocks=4):
    m, d = h2.shape
    ff = w_up.shape[1] // 2
    nf = ff // tf
    tm = min(tm, seq)
    tps = seq // tm
    assert conv_w.shape[0] == 3 and ff % tf == 0 and seq % tm == 0
    g, tails = pl.pallas_call(
        functools.partial(_ffn_up_kernel, tm=tm, tiles_per_seq=tps, sub_blocks=sub_blocks),
        grid=(m // tm, nf),
        in_specs=[pl.BlockSpec((tm, d), lambda i, f: (i, 0)),
                  pl.BlockSpec((HALO, d), lambda i, f: (jnp.maximum(i * (tm // HALO) - 1, 0), 0)),
                  pl.BlockSpec((d, tf), lambda i, f: (0, f)),
                  pl.BlockSpec((d, tf), lambda i, f: (0, nf + f)),
                  pl.BlockSpec((3, tf), lambda i, f: (0, f)),
                  pl.BlockSpec((1, tf), lambda i, f: (0, f))],
        out_specs=[pl.BlockSpec((tm, tf), lambda i, f: (i, f)),
                   pl.BlockSpec((None, 2, tf), lambda i, f: (i, 0, f))],
        out_shape=[jax.ShapeDtypeStruct((m, ff), BF16),
                   jax.ShapeDtypeStruct((m // tm, 2, ff), F32)],
        scratch_shapes=[pltpu.VMEM((HALO + tm, d), BF16), pltpu.VMEM((HALO + tm, tf), F32)],
        compiler_params=_params("arbitrary", "arbitrary"),
        name="ffn_up",
    )(h2, h2, w_up, w_up, conv_w, conv_b)
    return g, tails[tps - 1::tps]


def _ffn_down_kernel(g_ref, x_ref, g2_ref, wd_ref, fn_ref, y_ref, ss_scr, *, tn, sub_blocks):
    n = pl.program_id(1)
    d = y_ref.shape[1]

    @pl.when(n == 0)
    def _():
        ss_scr[...] = jnp.zeros_like(ss_scr)

    rows = y_ref.shape[0] // sub_blocks
    for r in range(sub_blocks):
        rs = slice(r * rows, (r + 1) * rows)
        x2 = x_ref[rs, :] + g2_ref[...] * _dot(g_ref[rs, :], wd_ref[...])
        ss_scr[rs, :] += jnp.sum(x2 * x2, axis=-1, keepdims=True)
        y_ref[rs, pl.ds(pl.multiple_of(n * tn, tn), tn)] = x2

    @pl.when(n == d // tn - 1)
    def _():
        y_ref[...] = y_ref[...] * lax.rsqrt(ss_scr[...] / d + EPS) * fn_ref[...]


def _ffn_down(g, x1, mod_p, w_down, final_norm, seq, tm=1024, tn=256, sub_blocks=4):
    m, d = x1.shape
    ff = w_down.shape[0]
    tm = min(tm, seq)
    tps = seq // tm
    nd = d // tn
    assert d % tn == 0 and seq % tm == 0
    return pl.pallas_call(
        functools.partial(_ffn_down_kernel, tn=tn, sub_blocks=sub_blocks),
        grid=(m // tm, nd),
        in_specs=[pl.BlockSpec((tm, ff), lambda i, n: (i, 0)),
                  pl.BlockSpec((tm, tn), lambda i, n: (i, n)),
                  pl.BlockSpec((None, 1, tn), lambda i, n: (i // tps, 0, 5 * nd + n)),
                  pl.BlockSpec((ff, tn), lambda i, n: (0, n)),
                  pl.BlockSpec((1, d), lambda i, n: (0, 0))],
        out_specs=pl.BlockSpec((tm, d), lambda i, n: (i, 0)),
        out_shape=jax.ShapeDtypeStruct((m, d), F32),
        scratch_shapes=[pltpu.VMEM((tm, 1), F32)],
        compiler_params=_params("arbitrary", "arbitrary"),
        name="ffn_down",
    )(g, x1, mod_p, w_down, final_norm)


def _ffn_sample_kernel(x_ref, h_ref, g2_ref, p0_ref, p1_ref, wa_ref, wu_ref, cw_ref,
                       cb_ref, wd_ref, fn_ref, y_ref, a_ref, acc_scr):
    f = pl.program_id(0)

    @pl.when(f == 0)
    def _():
        acc_scr[...] = jnp.zeros_like(acc_scr)

    a = _dot(h_ref[...], wa_ref[...])
    u = _dot(h_ref[...], wu_ref[...])
    conv = (cb_ref[...] + p0_ref[...] * cw_ref[0:1, :] + p1_ref[...] * cw_ref[1:2, :]
            + a * cw_ref[2:3, :])
    acc_scr[...] += _dot(_conv_gate(conv, u), wd_ref[...])
    a_ref[...] = a

    @pl.when(f == pl.num_programs(0) - 1)
    def _():
        y_ref[...] = _rms(x_ref[...] + g2_ref[...] * acc_scr[...]) * fn_ref[...]


def _ffn_sample(x1, h2, mod_s, prev2, w_up, conv_w, conv_b, w_down, final_norm, tf=512):
    m, d = x1.shape
    ff = w_down.shape[0]
    nf = ff // tf

    return pl.pallas_call(
        _ffn_sample_kernel,
        grid=(nf,),
        in_specs=[pl.BlockSpec((m, d), lambda f: (0, 0)),
                  pl.BlockSpec((m, d), lambda f: (0, 0)),
                  pl.BlockSpec((m, d), lambda f: (0, 5)),
                  pl.BlockSpec((m, tf), lambda f: (0, f)),
                  pl.BlockSpec((m, tf), lambda f: (0, nf + f)),
                  pl.BlockSpec((d, tf), lambda f: (0, f)),
                  pl.BlockSpec((d, tf), lambda f: (0, nf + f)),
                  pl.BlockSpec((3, tf), lambda f: (0, f)),
                  pl.BlockSpec((1, tf), lambda f: (0, f)),
                  pl.BlockSpec((tf, d), lambda f: (f, 0)),
                  pl.BlockSpec((1, d), lambda f: (0, 0))],
        out_specs=[pl.BlockSpec((m, d), lambda f: (0, 0)),
                   pl.BlockSpec((m, tf), lambda f: (0, f))],
        out_shape=[jax.ShapeDtypeStruct((m, d), F32),
                   jax.ShapeDtypeStruct((m, ff), F32)],
        scratch_shapes=[pltpu.VMEM((m, d), F32)],
        compiler_params=_params("arbitrary"),
        name="ffn_sample",
    )(x1, h2, mod_s, prev2, prev2, w_up, w_up, conv_w, conv_b, w_down, final_norm)


def _rope_tables(pos, qk_dim):
    rot = qk_dim // 4
    half = rot // 2
    inv_freq = ROPE_THETA ** (-jnp.arange(half, dtype=F32) / half)
    ang = pos.astype(F32)[:, None] * inv_freq[None, :]
    cos, sin = jnp.cos(ang), jnp.sin(ang)
    ones = jnp.ones((pos.shape[0], qk_dim - rot), F32)
    zeros = jnp.zeros((pos.shape[0], qk_dim - rot), F32)
    zh = jnp.zeros_like(sin)
    cos_t = jnp.concatenate([cos, cos, ones] * 2, axis=1)
    sin_a = jnp.concatenate([zh, sin, zeros] * 2, axis=1)
    sin_b = jnp.concatenate([-sin, zh, zeros] * 2, axis=1)
    return cos_t, sin_a, sin_b


def kernel(x_prompt, x_sample, c_prompt, c_sample, cache_k, cache_v, state_hgrn, state_conv, page_table,
           w_ada, b_ada, norm1, norm2, w_in, hg_lb_logits, hg_norm, lam_q1, lam_k1, lam_q2, lam_k2,
           at_norm, w_out, w_up, conv_w, conv_b, w_down, final_norm):
    n_batch, seq, d = x_prompt.shape
    n_seq = x_sample.shape[0]
    depth = w_in.shape[0]
    assert depth == 1 and x_sample.shape[1] == 1
    n_heads = state_hgrn.shape[2]
    assert cache_k.shape[3] == n_heads and cache_k.shape[4] == LANES and state_hgrn.shape[3] == LANES
    qk_dim = lam_q1.shape[1]
    past_len = page_table.shape[1] * cache_k.shape[2]
    lam_init = 0.8 - 0.6 * math.exp(-0.3 * 0)
    ff = w_down.shape[1]
    width = n_heads * LANES
    lam_params = [lam_q1, lam_k1, lam_q2, lam_k2]

    w_in16 = w_in[0].astype(BF16)
    w_out16 = w_out[0].astype(BF16)
    w_up16 = w_up[0].astype(BF16)
    w_down16 = w_down[0].astype(BF16)

    mod = _adaln(jnp.concatenate([c_prompt, c_sample], axis=0), w_ada[0], b_ada[0])
    mod_p = mod[:n_batch].reshape(n_batch, 1, N_MOD * d)
    mod_s = mod[n_batch:]

    tm_in = min(256, seq)
    tps_in = seq // tm_in
    tm_out = min(512, seq)
    tps_out = seq // tm_out
    xp = x_prompt.reshape(n_batch * seq, d)
    tables_p = _rope_tables(jnp.arange(seq, dtype=jnp.int32), qk_dim)

    def pspec(part):
        return pl.BlockSpec((None, 1, d), lambda i: (i // tps_in, 0, part))

    zq, k_p, v_p = _inproj_resident(xp, mod_p, mod_p, (pspec(0), pspec(1)), norm1, w_in16, tables_p,
                                    pl.BlockSpec((tm_in, LANES), lambda i: (i % tps_in, 0)), tm_in, n_heads)

    xs = x_sample.reshape(n_seq, d)
    tables_s = _rope_tables(jnp.full((n_seq,), past_len, jnp.int32), qk_dim)

    def sspec(part):
        return pl.BlockSpec((n_seq, d), lambda i, j: (0, part))

    zq_s, k_s, v_s = _inproj(xs, mod_s, mod_s, (sspec(0), sspec(1)), norm1, w_in16, tables_s,
                             pl.BlockSpec((n_seq, LANES), lambda i, j: (0, 0)), n_seq, n_heads, split=1)
    zq_s4 = zq_s.reshape(5, n_seq, n_heads, LANES)

    y_at_s, y_hg, hgrn_p = _pages_and_hgrn(
        page_table, lam_params, at_norm, zq_s4[4], k_s.reshape(n_seq, n_heads, LANES),
        v_s.reshape(n_seq, n_heads, LANES), cache_k, cache_v, lam_init,
        zq, hg_lb_logits, hg_norm, n_batch, seq)

    y_at = _attn_prompt(lam_params, at_norm, zq, k_p, v_p, n_batch, seq, n_heads, lam_init)
    x1, h2 = _outproj(xp, y_hg, y_at, mod_p,
                      lambda part: pl.BlockSpec((None, 1, d), lambda i: (i // tps_out, 0, part)),
                      norm2, w_out16, tm_out)
    g_p, conv_p = _ffn_up(h2, w_up16, conv_w[0], conv_b, seq)
    y_p = _ffn_down(g_p, x1, mod_p, w_down16, final_norm.reshape(1, d), seq)

    y_hg_s, hgrn_s = _hgrn_sample(zq_s4, hg_lb_logits.reshape(-1, n_heads, LANES), hg_norm, state_hgrn[0])
    x1_s, h2_s = _outproj(xs, y_hg_s.reshape(n_seq, width), y_at_s.reshape(n_seq, width), mod_s,
                          lambda part: pl.BlockSpec((n_seq, d), lambda i: (0, part)),
                          norm2, w_out16, n_seq)
    prev = state_conv[0]
    y_s, a_s = _ffn_sample(x1_s, h2_s, mod_s, prev.reshape(n_seq, 2 * ff), w_up16, conv_w[0], conv_b,
                           w_down16, final_norm.reshape(1, d))
    conv_s = jnp.stack([prev[:, 1, :], a_s], axis=1)

    return (y_p.reshape(n_batch, seq, d), y_s.reshape(n_seq, 1, d),
            k_p.reshape(1, n_batch, seq, n_heads, LANES), v_p.reshape(1, n_batch, seq, n_heads, LANES),
            k_s.reshape(1, n_seq, 1, n_heads, LANES), v_s.reshape(1, n_seq, 1, n_heads, LANES),
            hgrn_p[None], hgrn_s[None], conv_p[None], conv_s[None])
```
